```python
import math
import jax
import jax.numpy as jnp
from jax import lax
import numpy as np

D_MODEL = 1024
BATCH = 4
SEQ = 8192
DEPTH = 1

GRID_W = 64
CTX_LEN = 256
N_MOD = 6
NORM_EPS = 1e-6
FOURIER_GROUPS = 4
FOURIER_GROUP_DIM = 64
FOURIER_WIDTH = FOURIER_GROUPS * FOURIER_GROUP_DIM
DIFF_HEADS = 6
DIFF_HEAD_DIM = 64
DIFF_VALUE_DIM = 2 * DIFF_HEAD_DIM
DIFF_QK_WIDTH = DIFF_HEADS * 2 * DIFF_HEAD_DIM
DIFF_V_WIDTH = DIFF_HEADS * DIFF_VALUE_DIM
IN_WIDTH = FOURIER_WIDTH + 2 * DIFF_QK_WIDTH + DIFF_V_WIDTH
MIX_WIDTH = FOURIER_WIDTH + DIFF_V_WIDTH
QUERY_BLOCK = 128
ROPE_BASE = 10000.0
ROPE_AXIS_DIM = DIFF_HEAD_DIM // 2
N_EXPERTS = 64
TOP_K = 8
N_EXPERT_GROUPS = 8
TOP_K_GROUPS = 4
D_EXPERT = 256
ROUTED_SCALE = 2.5
EXPERT_BLOCK = 128

kernel_name = "hybrid_fourier_diffattn_moe_dit"


def rms_norm(x, g):
    xf = x.astype(jnp.float32)
    y = xf * lax.rsqrt(jnp.mean(xf * xf, axis=-1, keepdims=True) + NORM_EPS)
    return (y * g.astype(jnp.float32)).astype(x.dtype)


def modulate(h, shift, scale):
    return h * (1 + scale) + shift


def ada_params(cond, w_mod, b_mod):
    m = jax.nn.silu(cond) @ w_mod + b_mod
    return m.reshape(cond.shape[:-1] + (N_MOD, D_MODEL))


def axial_rope_tables(n_tokens):
    rows = n_tokens // GRID_W
    row = jnp.broadcast_to(jnp.arange(rows)[:, None], (rows, GRID_W)).reshape(-1)
    col = jnp.broadcast_to(jnp.arange(GRID_W)[None, :], (rows, GRID_W)).reshape(-1)
    inv_freq = ROPE_BASE ** (-jnp.arange(0, ROPE_AXIS_DIM, 2, dtype=jnp.float32) / ROPE_AXIS_DIM)
    pos = jnp.stack([row, col], axis=-1).astype(jnp.float32)
    ang = pos[:, :, None] * inv_freq
    return jnp.cos(ang), jnp.sin(ang)


def apply_axial_rope(t, cos, sin):
    xf = t.astype(jnp.float32).reshape(t.shape[:-1] + (2, 2, ROPE_AXIS_DIM // 2))
    x1 = xf[..., 0, :]
    x2 = xf[..., 1, :]
    cb = cos[None, :, None, None]
    sb = sin[None, :, None, None]
    out = jnp.stack([x1 * cb - x2 * sb, x2 * cb + x1 * sb], axis=-2)
    return out.reshape(t.shape).astype(t.dtype)


def split_groups(p):
    b, n, _ = p.shape
    o1 = FOURIER_WIDTH
    o2 = o1 + DIFF_QK_WIDTH
    o3 = o2 + DIFF_QK_WIDTH
    f = p[..., :o1]
    q = p[..., o1:o2].reshape(b, n, DIFF_HEADS, 2, DIFF_HEAD_DIM)
    k = p[..., o2:o3].reshape(b, n, DIFF_HEADS, 2, DIFF_HEAD_DIM)
    v = p[..., o3:].reshape(b, n, DIFF_HEADS, DIFF_VALUE_DIM)
    return f, q, k, v


def fourier_mix(f, w_fourier):
    b, n, _ = f.shape
    fg = f.astype(jnp.float32).reshape(b, n, FOURIER_GROUPS, FOURIER_GROUP_DIM)
    spec = jnp.fft.fft2(fg, axes=(1, 3), norm="ortho").real.astype(f.dtype)
    out = jnp.einsum("bngc,gcd->bngd", spec, w_fourier)
    return out.reshape(b, n, FOURIER_WIDTH)


def diff_attn_block(q, k, v, lam):
    s = jnp.einsum("bqhmd,bkhmd->bhmqk", q, k).astype(jnp.float32) * (DIFF_HEAD_DIM ** -0.5)
    p = jax.nn.softmax(s, axis=-1)
    a = p[:, :, 0] - lam * p[:, :, 1]
    return jnp.einsum("bhqk,bkhe->bqhe", a.astype(v.dtype), v)


def latent_diff_attention(qx, kx, vx, kc, vc, lam):
    b, n = qx.shape[:2]
    n_blocks = n // QUERY_BLOCK
    k_all = jnp.concatenate([kc, kx], axis=1)
    v_all = jnp.concatenate([vc, vx], axis=1)
    qb = qx.reshape(b, n_blocks, QUERY_BLOCK, DIFF_HEADS, 2, DIFF_HEAD_DIM).transpose(1, 0, 2, 3, 4, 5)
    o = lax.map(lambda q_blk: diff_attn_block(q_blk, k_all, v_all, lam), qb)
    return o.transpose(1, 0, 2, 3, 4).reshape(b, n, DIFF_HEADS, DIFF_VALUE_DIM)


def diff_subln(o, g, lam_init):
    b, n = o.shape[:2]
    return (rms_norm(o, g) * (1.0 - lam_init)).reshape(b, n, DIFF_V_WIDTH)


def swiglu(h, w_gate, w_up, w_down):
    return (jax.nn.silu(h @ w_gate) * (h @ w_up)) @ w_down


def moe_ffn(h, w_router, router_bias, w_gate, w_up, w_down, ws_gate, ws_up, ws_down):
    b, n, d = h.shape
    ht = h.reshape(b * n, d)
    n_tok = b * n
    scores = jax.nn.sigmoid((ht @ w_router).astype(jnp.float32))
    sel = scores + router_bias.astype(jnp.float32)
    grouped = sel.reshape(n_tok, N_EXPERT_GROUPS, N_EXPERTS // N_EXPERT_GROUPS)
    group_score = lax.top_k(grouped, 2)[0].sum(-1)
    gidx = lax.top_k(group_score, TOP_K_GROUPS)[1]
    group_mask = jnp.any(gidx[..., None] == jnp.arange(N_EXPERT_GROUPS), axis=-2)
    expert_mask = jnp.repeat(group_mask, N_EXPERTS // N_EXPERT_GROUPS, axis=-1)
    _, idx = lax.top_k(jnp.where(expert_mask, sel, -jnp.inf), TOP_K)
    wts = jnp.take_along_axis(scores, idx, axis=-1)
    wts = wts / (wts.sum(-1, keepdims=True) + 1e-20) * ROUTED_SCALE
    n_assign = n_tok * TOP_K
    n_blocks = -(-n_assign // EXPERT_BLOCK) + N_EXPERTS
    n_slots = n_blocks * EXPERT_BLOCK
    flat_e = idx.reshape(-1)
    flat_w = wts.reshape(-1).astype(h.dtype)
    flat_tok = jnp.arange(n_assign, dtype=jnp.int32) // TOP_K
    order = jnp.argsort(flat_e)
    e_sorted = flat_e[order]
    counts = jnp.zeros((N_EXPERTS,), jnp.int32).at[flat_e].add(1)
    start = jnp.cumsum(counts) - counts
    padded = (counts + EXPERT_BLOCK - 1) // EXPERT_BLOCK * EXPERT_BLOCK
    pad_end = jnp.cumsum(padded)
    pad_start = pad_end - padded
    dest = pad_start[e_sorted] + (jnp.arange(n_assign, dtype=jnp.int32) - start[e_sorted])
    slot_tok = jnp.full((n_slots,), n_tok, jnp.int32).at[dest].set(flat_tok[order])
    slot_w = jnp.zeros((n_slots,), h.dtype).at[dest].set(flat_w[order])
    block_expert = jnp.minimum(
        jnp.searchsorted(pad_end, jnp.arange(n_blocks, dtype=jnp.int32) * EXPERT_BLOCK, side="right"),
        N_EXPERTS - 1)
    h_pad = jnp.concatenate([ht, jnp.zeros((1, d), h.dtype)], axis=0)

    def expert_block(args):
        tok, w, e = args
        xb = h_pad[tok]
        return swiglu(xb, w_gate[e], w_up[e], w_down[e]) * w[:, None]

    y = lax.map(expert_block, (slot_tok.reshape(n_blocks, EXPERT_BLOCK),
                               slot_w.reshape(n_blocks, EXPERT_BLOCK), block_expert))
    routed = jnp.zeros((n_tok + 1, d), h.dtype).at[slot_tok].add(y.reshape(n_slots, d))[:n_tok]
    shared = swiglu(ht, ws_gate, ws_up, ws_down)
    return (routed + shared).reshape(b, n, d)


def setup_inputs(seed: int = 0) -> dict:
    key = jax.random.key(seed)
    ks = jax.random.split(key, 26)

    def nrm(k, shape, scale):
        return jax.random.normal(k, shape, jnp.float32) * scale

    def gain(k, shape):
        return 1.0 + 0.1 * jax.random.normal(k, shape, jnp.float32)

    d = D_MODEL
    return {
        "x": nrm(ks[0], (BATCH, SEQ, d), 1.0),
        "c": nrm(ks[1], (BATCH, d), 1.0),
        "ctx": nrm(ks[2], (BATCH, CTX_LEN, d), 1.0),
        "c_ctx": nrm(ks[3], (d,), 1.0),
        "w_mod": nrm(ks[4], (DEPTH, d, N_MOD * d), d ** -0.5),
        "b_mod": nrm(ks[5], (DEPTH, N_MOD * d), 0.02),
        "g_attn_pre": gain(ks[6], (DEPTH, d)),
        "g_attn_post": gain(ks[7], (DEPTH, d)),
        "w_in": nrm(ks[8], (DEPTH, d, IN_WIDTH), d ** -0.5),
        "w_fourier": nrm(ks[9], (DEPTH, FOURIER_GROUPS, FOURIER_GROUP_DIM, FOURIER_GROUP_DIM), FOURIER_GROUP_DIM ** -0.5),
        "lambda_q1": nrm(ks[10], (DEPTH, DIFF_HEAD_DIM), 0.1),
        "lambda_k1": nrm(ks[11], (DEPTH, DIFF_HEAD_DIM), 0.1),
        "lambda_q2": nrm(ks[12], (DEPTH, DIFF_HEAD_DIM), 0.1),
        "lambda_k2": nrm(ks[13], (DEPTH, DIFF_HEAD_DIM), 0.1),
        "g_subln": gain(ks[14], (DEPTH, DIFF_VALUE_DIM)),
        "w_out": nrm(ks[15], (DEPTH, MIX_WIDTH, d), MIX_WIDTH ** -0.5),
        "g_ffn_pre": gain(ks[16], (DEPTH, d)),
        "g_ffn_post": gain(ks[17], (DEPTH, d)),
        "w_router": nrm(ks[18], (DEPTH, d, N_EXPERTS), d ** -0.5),
        "router_bias": nrm(ks[19], (DEPTH, N_EXPERTS), 0.01),
        "w_gate": nrm(ks[20], (DEPTH, N_EXPERTS, d, D_EXPERT), d ** -0.5),
        "w_up": nrm(ks[21], (DEPTH, N_EXPERTS, d, D_EXPERT), d ** -0.5),
        "w_down": nrm(ks[22], (DEPTH, N_EXPERTS, D_EXPERT, d), D_EXPERT ** -0.5),
        "ws_gate": nrm(ks[23], (DEPTH, d, D_EXPERT), d ** -0.5),
        "ws_up": nrm(ks[24], (DEPTH, d, D_EXPERT), d ** -0.5),
        "ws_down": nrm(ks[25], (DEPTH, D_EXPERT, d), D_EXPERT ** -0.5),
    }


def reference(x, c, ctx, c_ctx, w_mod, b_mod, g_attn_pre, g_attn_post, w_in, w_fourier,
              lambda_q1, lambda_k1, lambda_q2, lambda_k2, g_subln, w_out, g_ffn_pre, g_ffn_post,
              w_router, router_bias, w_gate, w_up, w_down, ws_gate, ws_up, ws_down):
    n_lat = x.shape[1]
    cos, sin = axial_rope_tables(n_lat)
    for l in range(DEPTH):
        last = l == DEPTH - 1
        lam_init = 0.8 - 0.6 * math.exp(-0.3 * l)
        lam = (jnp.exp(jnp.sum(lambda_q1[l].astype(jnp.float32) * lambda_k1[l].astype(jnp.float32)))
               - jnp.exp(jnp.sum(lambda_q2[l].astype(jnp.float32) * lambda_k2[l].astype(jnp.float32)))
               + lam_init)
        mx = ada_params(c, w_mod[l], b_mod[l])[:, None]
        mc = ada_params(c_ctx, w_mod[l], b_mod[l])

        hx = modulate(rms_norm(x, g_attn_pre[l]), mx[..., 0, :], mx[..., 1, :])
        hc = modulate(rms_norm(ctx, g_attn_pre[l]), mc[..., 0, :], mc[..., 1, :])
        fx, qx, kx, vx = split_groups(hx @ w_in[l])
        fc, qc, kc, vc = split_groups(hc @ w_in[l])
        qx = apply_axial_rope(qx, cos, sin)
        kx = apply_axial_rope(kx, cos, sin)
        ox = latent_diff_attention(qx, kx, vx, kc, vc, lam)
        yx = jnp.concatenate([fourier_mix(fx, w_fourier[l]), diff_subln(ox, g_subln[l], lam_init)], axis=-1) @ w_out[l]
        x = x + mx[..., 2, :] * rms_norm(yx, g_attn_post[l])
        if not last:
            oc = diff_attn_block(qc, kc, vc, lam)
            yc = jnp.concatenate([fourier_mix(fc, w_fourier[l]), diff_subln(oc, g_subln[l], lam_init)], axis=-1) @ w_out[l]
            ctx = ctx + mc[..., 2, :] * rms_norm(yc, g_attn_post[l])

        moe_args = (w_router[l], router_bias[l], w_gate[l], w_up[l], w_down[l], ws_gate[l], ws_up[l], ws_down[l])
        hx = modulate(rms_norm(x, g_ffn_pre[l]), mx[..., 3, :], mx[..., 4, :])
        x = x + mx[..., 5, :] * rms_norm(moe_ffn(hx, *moe_args), g_ffn_post[l])
        if not last:
            hc = modulate(rms_norm(ctx, g_ffn_pre[l]), mc[..., 3, :], mc[..., 4, :])
            ctx = ctx + mc[..., 5, :] * rms_norm(moe_ffn(hc, *moe_args), g_ffn_post[l])
    return x
```

```python
import functools
import math

import numpy as np
import jax
import jax.numpy as jnp
from jax import lax
from jax.experimental import pallas as pl
from jax.experimental.pallas import tpu as pltpu

GRID_W = 64
N_MOD = 6
NORM_EPS = 1e-6
FOURIER_GROUPS = 4
FOURIER_GROUP_DIM = 64
FOURIER_WIDTH = FOURIER_GROUPS * FOURIER_GROUP_DIM
DIFF_HEADS = 6
DIFF_HEAD_DIM = 64
DIFF_VALUE_DIM = 2 * DIFF_HEAD_DIM
DIFF_QK_WIDTH = DIFF_HEADS * 2 * DIFF_HEAD_DIM
DIFF_V_WIDTH = DIFF_HEADS * DIFF_VALUE_DIM
ROPE_BASE = 10000.0
ROPE_AXIS_DIM = DIFF_HEAD_DIM // 2
N_EXPERTS = 64
TOP_K = 8
N_EXPERT_GROUPS = 8
TOP_K_GROUPS = 4
GROUP_SIZE = N_EXPERTS // N_EXPERT_GROUPS
ROUTED_SCALE = 2.5
LAM_INIT = 0.8 - 0.6 * math.exp(-0.3 * 0)

DFT_N1 = 64
DFT_KGROUP = 8
LANES = 128
VMEM_LIMIT = 56 * 1024 * 1024

BF16 = jnp.bfloat16
F32 = jnp.float32


def _cparams(sem):
    return pltpu.CompilerParams(dimension_semantics=sem, vmem_limit_bytes=VMEM_LIMIT)


def _bdot(a, b):
    return jnp.dot(a.astype(BF16), b.astype(BF16), preferred_element_type=F32)


def _rms(x):
    return x * lax.rsqrt(jnp.mean(x * x, axis=-1, keepdims=True) + NORM_EPS)


def _rope_tables(n_tokens):
    n = np.arange(n_tokens)
    pos = np.stack([n // GRID_W, n % GRID_W], axis=-1).astype(np.float32)
    inv_freq = (np.float32(ROPE_BASE) ** (-np.arange(0, ROPE_AXIS_DIM, 2, dtype=np.float32) / ROPE_AXIS_DIM)).astype(np.float32)
    lane = np.arange(LANES) % DIFF_HEAD_DIM
    axis = lane // ROPE_AXIS_DIM
    half = (lane % ROPE_AXIS_DIM) // (ROPE_AXIS_DIM // 2)
    freq = lane % (ROPE_AXIS_DIM // 2)
    ang = pos[:, axis] * inv_freq[freq][None, :]
    cos, sin = np.cos(ang), np.sin(ang)
    s_next = np.where(half[None, :] == 0, -sin, 0.0)
    s_prev = np.where(half[None, :] == 1, sin, 0.0)
    return (jnp.asarray(cos, F32), jnp.asarray(s_next, F32), jnp.asarray(s_prev, F32))


def _channel_dft(n_tokens):
    c = np.arange(FOURIER_GROUP_DIM)
    ang = 2.0 * np.pi * ((c[:, None] * c[None, :]) % FOURIER_GROUP_DIM) / FOURIER_GROUP_DIM
    norm = 1.0 / math.sqrt(n_tokens * FOURIER_GROUP_DIM)
    eye = np.eye(FOURIER_GROUPS)
    cc = np.kron(eye, np.cos(ang) * norm)
    sc = np.kron(eye, np.sin(ang) * norm)
    return jnp.asarray(cc, BF16), jnp.asarray(sc, BF16)


def _dft_stage1(n2):
    k = np.arange(n2)
    ang = 2.0 * np.pi * ((k[:, None] * k[None, :]) % n2) / n2
    c, s = np.cos(ang), np.sin(ang)
    m_u = np.concatenate([c, -s], axis=0)
    m_v = np.concatenate([-s, -c], axis=0)
    return jnp.asarray(m_u, BF16), jnp.asarray(m_v, BF16)


def _dft_stage2(n_tokens):
    n2 = n_tokens // DFT_N1
    groups = n2 // DFT_KGROUP
    k1 = np.arange(DFT_N1, dtype=np.int64)
    n1 = np.arange(DFT_N1, dtype=np.int64)
    g = np.arange(DFT_KGROUP, dtype=np.int64)
    a = np.zeros((groups, DFT_N1, DFT_KGROUP, DFT_KGROUP, DFT_N1), np.float64)
    b = np.zeros_like(a)
    for grp in range(groups):
        k2 = grp * DFT_KGROUP + g
        phase = (k1[:, None, None] * n1[None, None, :] * n2 + k2[None, :, None] * n1[None, None, :]) % n_tokens
        ang = 2.0 * np.pi * phase / n_tokens
        for gi in range(DFT_KGROUP):
            a[grp, :, gi, gi, :] = np.cos(ang[:, gi, :])
            b[grp, :, gi, gi, :] = np.sin(ang[:, gi, :])
    r = DFT_N1 * DFT_KGROUP
    return jnp.asarray(a.reshape(groups, r, r), BF16), jnp.asarray(b.reshape(groups, r, r), BF16)


def _ada_kernel(c_ref, w_ref, b_ref, o_ref):
    c = c_ref[...]
    s = c * jax.nn.sigmoid(c)
    o_ref[...] = jnp.dot(s, w_ref[...], preferred_element_type=F32) + b_ref[...]


def _ada(cond8, w_mod, b_mod):
    d = cond8.shape[1]
    return pl.pallas_call(
        _ada_kernel,
        out_shape=jax.ShapeDtypeStruct((8, N_MOD * d), F32),
        grid=(N_MOD,),
        in_specs=[pl.BlockSpec((8, d), lambda j: (0, 0)),
                  pl.BlockSpec((d, d), lambda j: (0, j)),
                  pl.BlockSpec((1, d), lambda j: (0, j))],
        out_specs=pl.BlockSpec((8, d), lambda j: (0, j)),
        compiler_params=_cparams(("arbitrary",)),
        name="ada",
    )(cond8, w_mod, b_mod.reshape(1, -1))


def _rope_chunk(ch, cos, s_next, s_prev):
    nxt = pltpu.roll(ch, LANES - ROPE_AXIS_DIM // 2, 1)
    prv = pltpu.roll(ch, ROPE_AXIS_DIM // 2, 1)
    return ch * cos + nxt * s_next + prv * s_prev


def _prep_latent_kernel(x_ref, mod_ref, g_ref, w_ref, cos_ref, sn_ref, sp_ref, cc_ref, sc_ref, wf_ref,
                        u_ref, v2_ref, q_ref, k_ref, v_ref):
    x = x_ref[0]
    h = _rms(x) * g_ref[...]
    h = h * (1.0 + mod_ref[0, 1:2, :]) + mod_ref[0, 0:1, :]
    hb = h.astype(BF16)
    o1 = FOURIER_WIDTH
    o2 = o1 + DIFF_QK_WIDTH
    o3 = o2 + DIFF_QK_WIDTH
    f = jnp.dot(hb, w_ref[:, 0:o1], preferred_element_type=F32).astype(BF16)
    fc = jnp.dot(f, cc_ref[...], preferred_element_type=F32).astype(BF16)
    fs = jnp.dot(f, sc_ref[...], preferred_element_type=F32).astype(BF16)
    u_ref[0] = jnp.dot(fc, wf_ref[...], preferred_element_type=F32).astype(BF16)
    v2_ref[0] = jnp.dot(fs, wf_ref[...], preferred_element_type=F32).astype(BF16)
    cos, sn, sp = cos_ref[...], sn_ref[...], sp_ref[...]
    for j in range(DIFF_QK_WIDTH // LANES):
        lo = j * LANES
        qc = jnp.dot(hb, w_ref[:, o1 + lo:o1 + lo + LANES], preferred_element_type=F32)
        q_ref[0, :, lo:lo + LANES] = (_rope_chunk(qc, cos, sn, sp) * (DIFF_HEAD_DIM ** -0.5)).astype(BF16)
        kc = jnp.dot(hb, w_ref[:, o2 + lo:o2 + lo + LANES], preferred_element_type=F32)
        k_ref[0, :, lo:lo + LANES] = _rope_chunk(kc, cos, sn, sp).astype(BF16)
    v_ref[0] = jnp.dot(hb, w_ref[:, o3:], preferred_element_type=F32).astype(BF16)


def _prep_latent(x, mod, g, w_in_b, rope, cc, sc, wf, tm):
    b, n, d = x.shape
    cos, sn, sp = rope
    tok = lambda w: pl.BlockSpec((1, tm, w), lambda bi, i: (bi, i, 0))
    full = lambda a: pl.BlockSpec(a.shape, lambda bi, i: (0,) * a.ndim)
    tab = pl.BlockSpec((tm, LANES), lambda bi, i: (i, 0))
    outs = [jax.ShapeDtypeStruct((b, n, w), BF16) for w in (FOURIER_WIDTH, FOURIER_WIDTH, DIFF_QK_WIDTH, DIFF_QK_WIDTH, DIFF_V_WIDTH)]
    return pl.pallas_call(
        _prep_latent_kernel,
        out_shape=outs,
        grid=(b, n // tm),
        in_specs=[tok(d), pl.BlockSpec((1, N_MOD, d), lambda bi, i: (bi, 0, 0)), full(g), full(w_in_b),
                  tab, tab, tab, full(cc), full(sc), full(wf)],
        out_specs=[tok(FOURIER_WIDTH), tok(FOURIER_WIDTH), tok(DIFF_QK_WIDTH), tok(DIFF_QK_WIDTH), tok(DIFF_V_WIDTH)],
        compiler_params=_cparams(("parallel", "parallel")),
        name="prep_latent",
    )(x, mod, g, w_in_b, cos, sn, sp, cc, sc, wf)


def _prep_ctx_kernel(x_ref, mod_ref, g_ref, wk_ref, wv_ref, k_ref, v_ref):
    x = x_ref[0]
    h = _rms(x) * g_ref[...]
    h = h * (1.0 + mod_ref[0, 1:2, :]) + mod_ref[0, 0:1, :]
    hb = h.astype(BF16)
    k_ref[0] = jnp.dot(hb, wk_ref[...], preferred_element_type=F32).astype(BF16)
    v_ref[0] = jnp.dot(hb, wv_ref[...], preferred_element_type=F32).astype(BF16)


def _prep_ctx(ctx, mod, g, wk, wv, ctx_row):
    b, n, d = ctx.shape
    tok = lambda w: pl.BlockSpec((1, n, w), lambda bi: (bi, 0, 0))
    full = lambda a: pl.BlockSpec(a.shape, lambda bi: (0,) * a.ndim)
    return pl.pallas_call(
        _prep_ctx_kernel,
        out_shape=[jax.ShapeDtypeStruct((b, n, DIFF_QK_WIDTH), BF16), jax.ShapeDtypeStruct((b, n, DIFF_V_WIDTH), BF16)],
        grid=(b,),
        in_specs=[tok(d), pl.BlockSpec((1, N_MOD, d), lambda bi: (ctx_row, 0, 0)), full(g), full(wk), full(wv)],
        out_specs=[tok(DIFF_QK_WIDTH), tok(DIFF_V_WIDTH)],
        compiler_params=_cparams(("parallel",)),
        name="prep_ctx",
    )(ctx, mod, g, wk, wv)


def _dft1_kernel(u_ref, v_ref, mu_ref, mv_ref, yr_ref, yi_ref):
    n2 = u_ref.shape[1]
    y = (jnp.dot(mu_ref[...], u_ref[0], preferred_element_type=F32)
         + jnp.dot(mv_ref[...], v_ref[0], preferred_element_type=F32))
    yr_ref[0] = y[:n2].astype(BF16)
    yi_ref[0] = y[n2:].astype(BF16)


def _dft2_kernel(yr_ref, yi_ref, a_ref, b_ref, o_ref):
    y = (jnp.dot(a_ref[0], yr_ref[0], preferred_element_type=F32)
         + jnp.dot(b_ref[0], yi_ref[0], preferred_element_type=F32))
    o_ref[0] = y.reshape(DFT_N1, DFT_KGROUP, FOURIER_WIDTH).astype(BF16)


def _fourier(u, v, m_u, m_v, a2, b2):
    b, n, w = u.shape
    n2 = n // DFT_N1
    cols = DFT_N1 * w
    tn = min(cols, 2048)
    u2 = u.reshape(b, n2, cols)
    v2 = v.reshape(b, n2, cols)
    blk = pl.BlockSpec((1, n2, tn), lambda bi, j: (bi, 0, j))
    mat = pl.BlockSpec((2 * n2, n2), lambda bi, j: (0, 0))
    yr, yi = pl.pallas_call(
        _dft1_kernel,
        out_shape=[jax.ShapeDtypeStruct((b, n2, cols), BF16)] * 2,
        grid=(b, cols // tn),
        in_specs=[blk, blk, mat, mat],
        out_specs=[blk, blk],
        compiler_params=_cparams(("parallel", "parallel")),
        name="dft_stage1",
    )(u2, v2, m_u, m_v)
    yr = yr.reshape(b, n, w)
    yi = yi.reshape(b, n, w)
    rows = DFT_N1 * DFT_KGROUP
    groups = n2 // DFT_KGROUP
    yblk = pl.BlockSpec((1, rows, w), lambda bi, gi: (bi, gi, 0))
    mblk = pl.BlockSpec((1, rows, rows), lambda bi, gi: (gi, 0, 0))
    out = pl.pallas_call(
        _dft2_kernel,
        out_shape=jax.ShapeDtypeStruct((b, DFT_N1, n2, w), BF16),
        grid=(b, groups),
        in_specs=[yblk, yblk, mblk, mblk],
        out_specs=pl.BlockSpec((1, DFT_N1, DFT_KGROUP, w), lambda bi, gi: (bi, 0, gi, 0)),
        compiler_params=_cparams(("parallel", "parallel")),
        name="dft_stage2",
    )(yr, yi, a2, b2)
    return out.reshape(b, n, w)


def _attn_kernel(q_ref, kc_ref, vc_ref, kx_ref, vx_ref, lam_ref, g_ref, o_ref,
                 m_ref, l_ref, acc_ref, *, tk):
    tq = q_ref.shape[1]
    n_lat = kx_ref.shape[1]
    q = q_ref[0]
    lane = lax.broadcasted_iota(jnp.int32, (tq, LANES), 1)
    zero = jnp.zeros_like(q)
    qh = (jnp.where(lane < DIFF_HEAD_DIM, q, zero), jnp.where(lane >= DIFF_HEAD_DIM, q, zero))
    m_ref[...] = jnp.full(m_ref.shape, -jnp.inf, F32)
    l_ref[...] = jnp.zeros(l_ref.shape, F32)
    acc_ref[...] = jnp.zeros(acc_ref.shape, F32)

    def chunk(k, v):
        for half in range(2):
            s = lax.dot_general(qh[half], k, (((1,), (1,)), ((), ())), preferred_element_type=F32)
            m_old = m_ref[half]
            m_new = jnp.maximum(m_old, jnp.max(s, axis=-1, keepdims=True))
            alpha = jnp.exp(m_old - m_new)
            p = jnp.exp(s - m_new)
            l_ref[half] = alpha * l_ref[half] + jnp.sum(p, axis=-1, keepdims=True)
            acc_ref[half] = alpha * acc_ref[half] + jnp.dot(p.astype(BF16), v, preferred_element_type=F32)
            m_ref[half] = m_new

    chunk(kc_ref[0], vc_ref[0])

    def body(j, carry):
        start = pl.multiple_of(j * tk, tk)
        chunk(kx_ref[0, pl.ds(start, tk), :], vx_ref[0, pl.ds(start, tk), :])
        return carry

    lax.fori_loop(0, n_lat // tk, body, 0)

    lam = (jnp.exp(jnp.sum(lam_ref[0:1, :] * lam_ref[1:2, :], axis=-1, keepdims=True))
           - jnp.exp(jnp.sum(lam_ref[2:3, :] * lam_ref[3:4, :], axis=-1, keepdims=True)) + LAM_INIT)
    o = acc_ref[0] / l_ref[0] - lam * (acc_ref[1] / l_ref[1])
    o_ref[0] = (_rms(o) * g_ref[...] * (1.0 - LAM_INIT)).astype(BF16)


def _attention(q, kc, vc, kx, vx, lam_vecs, g_subln, tq, tk):
    b, n, _ = q.shape
    n_ctx = kc.shape[1]
    qspec = pl.BlockSpec((1, tq, LANES), lambda bi, h, i: (bi, i, h))
    cspec = pl.BlockSpec((1, n_ctx, LANES), lambda bi, h, i: (bi, 0, h))
    xspec = pl.BlockSpec((1, n, LANES), lambda bi, h, i: (bi, 0, h))
    return pl.pallas_call(
        functools.partial(_attn_kernel, tk=tk),
        out_shape=jax.ShapeDtypeStruct((b, n, DIFF_V_WIDTH), BF16),
        grid=(b, DIFF_HEADS, n // tq),
        in_specs=[qspec, cspec, cspec, xspec, xspec,
                  pl.BlockSpec(lam_vecs.shape, lambda bi, h, i: (0, 0)),
                  pl.BlockSpec(g_subln.shape, lambda bi, h, i: (0, 0))],
        out_specs=qspec,
        scratch_shapes=[pltpu.VMEM((2, tq, 1), F32), pltpu.VMEM((2, tq, 1), F32), pltpu.VMEM((2, tq, LANES), F32)],
        compiler_params=_cparams(("parallel", "parallel", "arbitrary")),
        name="diff_attention",
    )(q, kc, vc, kx, vx, lam_vecs, g_subln)


def _route(logits_t, bias_col):
    tm = logits_t.shape[1]
    scores = jax.nn.sigmoid(logits_t)
    sel = scores + bias_col
    ninf = jnp.float32(-jnp.inf)
    i8 = lax.broadcasted_iota(jnp.int32, (GROUP_SIZE, tm), 0)
    slabs = [sel[GROUP_SIZE * g:GROUP_SIZE * (g + 1)] for g in range(N_EXPERT_GROUPS)]
    rows = []
    for slab in slabs:
        m1 = jnp.max(slab, axis=0, keepdims=True)
        i1 = jnp.min(jnp.where(slab == m1, i8, GROUP_SIZE), axis=0, keepdims=True)
        m2 = jnp.max(jnp.where(i8 == i1, ninf, slab), axis=0, keepdims=True)
        rows.append(m1 + m2)
    gs = jnp.concatenate(rows, axis=0)
    gmask = jnp.zeros(gs.shape, jnp.bool_)
    for _ in range(TOP_K_GROUPS):
        m = jnp.max(gs, axis=0, keepdims=True)
        i = jnp.min(jnp.where(gs == m, i8, N_EXPERT_GROUPS), axis=0, keepdims=True)
        hit = i8 == i
        gmask = jnp.logical_or(gmask, hit)
        gs = jnp.where(hit, ninf, gs)
    gmask_f = gmask.astype(F32)
    cand = [jnp.where(gmask_f[g:g + 1] > 0.5, slabs[g], ninf) for g in range(N_EXPERT_GROUPS)]
    chosen = [jnp.zeros((GROUP_SIZE, tm), jnp.bool_) for _ in range(N_EXPERT_GROUPS)]
    for _ in range(TOP_K):
        mm = cand[0]
        for g in range(1, N_EXPERT_GROUPS):
            mm = jnp.maximum(mm, cand[g])
        m = jnp.max(mm, axis=0, keepdims=True)
        ii = jnp.where(cand[0] == m, i8, N_EXPERTS)
        for g in range(1, N_EXPERT_GROUPS):
            ii = jnp.minimum(ii, jnp.where(cand[g] == m, i8 + GROUP_SIZE * g, N_EXPERTS))
        idx = jnp.min(ii, axis=0, keepdims=True)
        for g in range(N_EXPERT_GROUPS):
            hit = (i8 + GROUP_SIZE * g) == idx
            chosen[g] = jnp.logical_or(chosen[g], hit)
            cand[g] = jnp.where(hit, ninf, cand[g])
    picked = [jnp.where(chosen[g], scores[GROUP_SIZE * g:GROUP_SIZE * (g + 1)], 0.0) for g in range(N_EXPERT_GROUPS)]
    tot = picked[0]
    for g in range(1, N_EXPERT_GROUPS):
        tot = tot + picked[g]
    denom = jnp.sum(tot, axis=0, keepdims=True) + 1e-20
    return jnp.concatenate([p / denom * ROUTED_SCALE for p in picked], axis=0)


def _mix_kernel(four_ref, att_ref, x_ref, mod_ref, gpost_ref, gpre_ref, wo_ref, wr_ref, rb_ref,
                wsg_ref, wsu_ref, wsd_ref, x1_ref, h2_ref, sh_ref, gate_ref):
    y = (jnp.dot(four_ref[0], wo_ref[0:FOURIER_WIDTH, :], preferred_element_type=F32)
         + jnp.dot(att_ref[0], wo_ref[FOURIER_WIDTH:, :], preferred_element_type=F32))
    x1 = x_ref[0] + mod_ref[0, 2:3, :] * (_rms(y) * gpost_ref[...])
    x1_ref[0] = x1
    h2 = _rms(x1) * gpre_ref[...]
    h2 = h2 * (1.0 + mod_ref[0, 4:5, :]) + mod_ref[0, 3:4, :]
    hb = h2.astype(BF16)
    h2_ref[0] = hb
    a = jnp.dot(hb, wsg_ref[...], preferred_element_type=F32)
    u = jnp.dot(hb, wsu_ref[...], preferred_element_type=F32)
    hid = (a * jax.nn.sigmoid(a)) * u
    sh_ref[0] = jnp.dot(hid.astype(BF16), wsd_ref[...], preferred_element_type=F32)
    logits = jnp.dot(h2, wr_ref[...], preferred_element_type=F32, precision=lax.Precision.HIGHEST)
    gate_ref[0] = _route(logits.T, rb_ref[...]).T


def _mix(four, att, x, mod, g_post, g_pre, w_out_b, w_router, rb_col, wsg, wsu, wsd, tm):
    b, n, d = x.shape
    tok = lambda w: pl.BlockSpec((1, tm, w), lambda bi, i: (bi, i, 0))
    full = lambda a: pl.BlockSpec(a.shape, lambda bi, i: (0,) * a.ndim)
    return pl.pallas_call(
        _mix_kernel,
        out_shape=[jax.ShapeDtypeStruct((b, n, d), F32), jax.ShapeDtypeStruct((b, n, d), BF16),
                   jax.ShapeDtypeStruct((b, n, d), F32), jax.ShapeDtypeStruct((b, n, N_EXPERTS), F32)],
        grid=(b, n // tm),
        in_specs=[tok(FOURIER_WIDTH), tok(DIFF_V_WIDTH), tok(d), pl.BlockSpec((1, N_MOD, d), lambda bi, i: (bi, 0, 0)),
                  full(g_post), full(g_pre), full(w_out_b), full(w_router), full(rb_col), full(wsg), full(wsu), full(wsd)],
        out_specs=[tok(d), tok(d), tok(d), tok(N_EXPERTS)],
        compiler_params=_cparams(("parallel", "parallel")),
        name="mix",
    )(four, att, x, mod, g_post, g_pre, w_out_b, w_router, rb_col, wsg, wsu, wsd)


def _moe_kernel(h_ref, gate_ref, sh_ref, x1_ref, mod_ref, g_ref, wg_ref, wu_ref, wd_ref, o_ref, acc_ref):
    e = pl.program_id(2)

    @pl.when(e == 0)
    def _():
        acc_ref[...] = sh_ref[0]

    h = h_ref[0]
    a = jnp.dot(h, wg_ref[0], preferred_element_type=F32)
    u = jnp.dot(h, wu_ref[0], preferred_element_type=F32)
    lane = lax.broadcasted_iota(jnp.int32, gate_ref.shape[1:], 1)
    gcol = jnp.sum(jnp.where(lane == e, gate_ref[0], 0.0), axis=-1, keepdims=True)
    hid = (a * jax.nn.sigmoid(a)) * u * gcol
    acc_ref[...] += jnp.dot(hid.astype(BF16), wd_ref[0], preferred_element_type=F32)

    @pl.when(e == N_EXPERTS - 1)
    def _():
        o_ref[0] = x1_ref[0] + mod_ref[0, 5:6, :] * (_rms(acc_ref[...]) * g_ref[...])


def _moe(h2, gate, shared, x1, mod, g_post, wg, wu, wd, tm):
    b, n, d = x1.shape
    de = wg.shape[-1]
    tok = lambda w: pl.BlockSpec((1, tm, w), lambda bi, i, e: (bi, i, 0))
    return pl.pallas_call(
        _moe_kernel,
        out_shape=jax.ShapeDtypeStruct((b, n, d), F32),
        grid=(b, n // tm, N_EXPERTS),
        in_specs=[tok(d), tok(N_EXPERTS), tok(d), tok(d), pl.BlockSpec((1, N_MOD, d), lambda bi, i, e: (bi, 0, 0)),
                  pl.BlockSpec(g_post.shape, lambda bi, i, e: (0, 0)),
                  pl.BlockSpec((1, d, de), lambda bi, i, e: (e, 0, 0)),
                  pl.BlockSpec((1, d, de), lambda bi, i, e: (e, 0, 0)),
                  pl.BlockSpec((1, de, d), lambda bi, i, e: (e, 0, 0))],
        out_specs=tok(d),
        scratch_shapes=[pltpu.VMEM((tm, d), F32)],
        compiler_params=_cparams(("parallel", "parallel", "arbitrary")),
        name="moe_dense",
    )(h2, gate, shared, x1, mod, g_post, wg, wu, wd)


def kernel(x, c, ctx, c_ctx, w_mod, b_mod, g_attn_pre, g_attn_post, w_in, w_fourier, lambda_q1, lambda_k1,
           lambda_q2, lambda_k2, g_subln, w_out, g_ffn_pre, g_ffn_post, w_router, router_bias, w_gate, w_up,
           w_down, ws_gate, ws_up, ws_down):
    b, n, d = x.shape
    assert w_mod.shape[0] == 1, "single-layer kernel"
    assert n % (DFT_N1 * DFT_KGROUP) == 0 and b + 1 <= 8
    row = lambda a: a[0].reshape(1, -1)

    cond8 = jnp.zeros((8, d), F32).at[:b].set(c).at[b].set(c_ctx)
    mod = _ada(cond8, w_mod[0], b_mod[0]).reshape(8, N_MOD, d)

    w_in_b = w_in[0].astype(BF16)
    o2 = FOURIER_WIDTH + DIFF_QK_WIDTH
    o3 = o2 + DIFF_QK_WIDTH
    cc, sc = _channel_dft(n)
    wf = jnp.zeros((FOURIER_GROUPS, FOURIER_GROUP_DIM, FOURIER_GROUPS, FOURIER_GROUP_DIM), F32)
    for g in range(FOURIER_GROUPS):
        wf = wf.at[g, :, g, :].set(w_fourier[0, g])
    wf = wf.reshape(FOURIER_WIDTH, FOURIER_WIDTH).astype(BF16)

    tm = min(n, 512)
    u, v2, q, kx, vx = _prep_latent(x, mod, row(g_attn_pre), w_in_b, _rope_tables(n), cc, sc, wf, tm)
    kc, vc = _prep_ctx(ctx, mod, row(g_attn_pre), w_in_b[:, o2:o3], w_in_b[:, o3:], b)

    m_u, m_v = _dft_stage1(n // DFT_N1)
    a2, b2 = _dft_stage2(n)
    four = _fourier(u, v2, m_u, m_v, a2, b2)

    lam_vecs = jnp.concatenate([lambda_q1, lambda_k1, lambda_q2, lambda_k2], axis=0).astype(F32)
    att = _attention(q, kc, vc, kx, vx, lam_vecs, row(g_subln), min(n, 512), min(n, 512))

    x1, h2, shared, gate = _mix(four, att, x, mod, row(g_attn_post), row(g_ffn_pre), w_out[0].astype(BF16),
                                w_router[0], router_bias[0].reshape(-1, 1), ws_gate[0].astype(BF16),
                                ws_up[0].astype(BF16), ws_down[0].astype(BF16), min(n, 512))
    return _moe(h2, gate, shared, x1, mod, row(g_ffn_post), w_gate[0].astype(BF16), w_up[0].astype(BF16),
                w_down[0].astype(BF16), min(n, 1024))
```

```python
import functools
import math

import numpy as np
import jax
import jax.numpy as jnp
from jax import lax
from jax.experimental import pallas as pl
from jax.experimental.pallas import tpu as pltpu

GRID_W = 64
N_MOD = 6
NORM_EPS = 1e-6
FOURIER_GROUPS = 4
FOURIER_GROUP_DIM = 64
FOURIER_WIDTH = FOURIER_GROUPS * FOURIER_GROUP_DIM
DIFF_HEADS = 6
DIFF_HEAD_DIM = 64
DIFF_VALUE_DIM = 2 * DIFF_HEAD_DIM
DIFF_QK_WIDTH = DIFF_HEADS * 2 * DIFF_HEAD_DIM
DIFF_V_WIDTH = DIFF_HEADS * DIFF_VALUE_DIM
ROPE_BASE = 10000.0
ROPE_AXIS_DIM = DIFF_HEAD_DIM // 2
N_EXPERTS = 64
TOP_K = 8
N_EXPERT_GROUPS = 8
TOP_K_GROUPS = 4
GROUP_SIZE = N_EXPERTS // N_EXPERT_GROUPS
ROUTED_SCALE = 2.5
LAM_INIT = 0.8 - 0.6 * math.exp(-0.3 * 0)
Q_SCALE = DIFF_HEAD_DIM ** -0.5

DFT_N1 = 64
DFT_KGROUP = 8
LANES = 128
VMEM_LIMIT = 56 * 1024 * 1024

BF16 = jnp.bfloat16
F32 = jnp.float32


MAX_UNROLLED_KV_CHUNKS = 4
ATTN_KV_TARGET = 2816


def _tiles(n, n_ctx):
    n_keys = n + n_ctx
    kv = max(t for t in range(LANES, n_keys + 1, LANES) if n_keys % t == 0 and t <= ATTN_KV_TARGET)
    return {"prep": min(n, n_ctx, 256), "attn_q": min(n, 512), "attn_kv": kv, "mix": min(n, 512), "moe": min(n, 1024)}


def _cparams(sem):
    return pltpu.CompilerParams(dimension_semantics=sem, vmem_limit_bytes=VMEM_LIMIT)


def _bdot(a, b):
    return jnp.dot(a.astype(BF16), b.astype(BF16), preferred_element_type=F32)


def _rms(x):
    return x * lax.rsqrt(jnp.mean(x * x, axis=-1, keepdims=True) + NORM_EPS)


_TWO_PI_HI = float(np.float32(2.0 * np.pi))
_TWO_PI_LO = float(np.float32(2.0 * np.pi - _TWO_PI_HI))


def _cos_sin(phase, period):
    t = phase.astype(F32) / period
    ang = t * _TWO_PI_HI + t * _TWO_PI_LO
    return jnp.cos(ang), jnp.sin(ang)


def _rope_tables(n_tokens):
    rows = n_tokens // GRID_W
    row = jnp.broadcast_to(jnp.arange(rows)[:, None], (rows, GRID_W)).reshape(-1)
    col = jnp.broadcast_to(jnp.arange(GRID_W)[None, :], (rows, GRID_W)).reshape(-1)
    inv_freq = ROPE_BASE ** (-jnp.arange(0, ROPE_AXIS_DIM, 2, dtype=F32) / ROPE_AXIS_DIM)
    pos = jnp.stack([row, col], axis=-1).astype(F32)
    ang = pos[:, :, None] * inv_freq
    cos, sin = jnp.cos(ang), jnp.sin(ang)
    z = jnp.zeros_like(sin[:, 0])
    reps = LANES // DIFF_HEAD_DIM
    cos_l = jnp.tile(jnp.concatenate([cos[:, 0], cos[:, 0], cos[:, 1], cos[:, 1]], axis=-1), (1, reps))
    s_next = jnp.tile(jnp.concatenate([-sin[:, 0], z, -sin[:, 1], z], axis=-1), (1, reps))
    s_prev = jnp.tile(jnp.concatenate([z, sin[:, 0], z, sin[:, 1]], axis=-1), (1, reps))
    return cos_l, s_next, s_prev


def _channel_dft(n_tokens):
    c = jnp.arange(FOURIER_GROUP_DIM, dtype=jnp.int32)
    cs, sn = _cos_sin((c[:, None] * c[None, :]) % FOURIER_GROUP_DIM, FOURIER_GROUP_DIM)
    norm = lax.rsqrt(jnp.float32(n_tokens * FOURIER_GROUP_DIM))
    eye = jnp.eye(FOURIER_GROUPS, dtype=F32)
    return jnp.kron(eye, cs * norm).astype(BF16), jnp.kron(eye, sn * norm).astype(BF16)


def _dft_stage1(n2):
    k = jnp.arange(n2, dtype=jnp.int32)
    c, s = _cos_sin((k[:, None] * k[None, :]) % n2, n2)
    m_u = jnp.concatenate([c, -s], axis=0)
    m_v = jnp.concatenate([-s, -c], axis=0)
    return m_u.astype(BF16), m_v.astype(BF16)


def _dft_stage2(n_tokens):
    n2 = n_tokens // DFT_N1
    groups = n2 // DFT_KGROUP
    k1 = jnp.arange(DFT_N1, dtype=jnp.int32)[None, :, None, None]
    n1 = jnp.arange(DFT_N1, dtype=jnp.int32)[None, None, None, :]
    k2 = (jnp.arange(groups, dtype=jnp.int32)[:, None, None, None] * DFT_KGROUP
          + jnp.arange(DFT_KGROUP, dtype=jnp.int32)[None, None, :, None])
    c, s = _cos_sin((k1 * n1 * n2 + k2 * n1) % n_tokens, n_tokens)
    eye = jnp.eye(DFT_KGROUP, dtype=F32)[None, None, :, :, None]
    r = DFT_N1 * DFT_KGROUP
    a = (c[:, :, :, None, :] * eye).reshape(groups, r, r)
    b = (s[:, :, :, None, :] * eye).reshape(groups, r, r)
    return a.astype(BF16), b.astype(BF16)


def _ada_kernel(c_ref, w_ref, b_ref, o_ref):
    c = c_ref[...]
    s = c * jax.nn.sigmoid(c)
    o_ref[...] = jnp.dot(s, w_ref[...], preferred_element_type=F32) + b_ref[...]


def _ada(cond8, w_mod, b_mod):
    d = cond8.shape[1]
    return pl.pallas_call(
        _ada_kernel,
        out_shape=jax.ShapeDtypeStruct((8, N_MOD * d), F32),
        grid=(N_MOD,),
        in_specs=[pl.BlockSpec((8, d), lambda j: (0, 0)),
                  pl.BlockSpec((d, d), lambda j: (0, j)),
                  pl.BlockSpec((1, d), lambda j: (0, j))],
        out_specs=pl.BlockSpec((8, d), lambda j: (0, j)),
        compiler_params=_cparams(("arbitrary",)),
        name="ada",
    )(cond8, w_mod, b_mod.reshape(1, -1))


def _rope_chunk(ch, cos, s_next, s_prev):
    nxt = pltpu.roll(ch, LANES - ROPE_AXIS_DIM // 2, 1)
    prv = pltpu.roll(ch, ROPE_AXIS_DIM // 2, 1)
    return ch * cos + nxt * s_next + prv * s_prev


def _prep_latent_kernel(x_ref, mod_ref, g_ref, w_ref, cos_ref, sn_ref, sp_ref, cc_ref, sc_ref, wf_ref,
                        k_all_ref, v_all_ref, u_ref, v2_ref, q_ref, k_ref, v_ref):
    del k_all_ref, v_all_ref
    x = x_ref[0]
    h = _rms(x) * g_ref[...]
    h = h * (1.0 + mod_ref[0, 1:2, :]) + mod_ref[0, 0:1, :]
    hb = h.astype(BF16)
    o1 = FOURIER_WIDTH
    o2 = o1 + DIFF_QK_WIDTH
    o3 = o2 + DIFF_QK_WIDTH
    f = jnp.dot(hb, w_ref[:, 0:o1], preferred_element_type=F32).astype(BF16)
    fc = jnp.dot(f, cc_ref[...], preferred_element_type=F32).astype(BF16)
    fs = jnp.dot(f, sc_ref[...], preferred_element_type=F32).astype(BF16)
    u_ref[0] = jnp.dot(fc, wf_ref[...], preferred_element_type=F32).astype(BF16)
    v2_ref[0] = jnp.dot(fs, wf_ref[...], preferred_element_type=F32).astype(BF16)
    cos, sn, sp = cos_ref[...], sn_ref[...], sp_ref[...]
    for j in range(DIFF_QK_WIDTH // LANES):
        lo = j * LANES
        qc = jnp.dot(hb, w_ref[:, o1 + lo:o1 + lo + LANES], preferred_element_type=F32)
        q_ref[0, :, lo:lo + LANES] = (_rope_chunk(qc, cos, sn, sp) * Q_SCALE).astype(BF16)
        kc = jnp.dot(hb, w_ref[:, o2 + lo:o2 + lo + LANES], preferred_element_type=F32)
        k_ref[0, :, lo:lo + LANES] = _rope_chunk(kc, cos, sn, sp).astype(BF16)
    v_ref[0] = jnp.dot(hb, w_ref[:, o3:], preferred_element_type=F32).astype(BF16)


def _prep_latent(x, mod, g, w_in_b, rope, cc, sc, wf, k_all, v_all, tm):
    b, n, d = x.shape
    n_ctx = k_all.shape[1] - n
    assert n_ctx % tm == 0
    cos, sn, sp = rope
    tok = lambda w: pl.BlockSpec((1, tm, w), lambda bi, i: (bi, i, 0))
    behind = lambda w: pl.BlockSpec((1, tm, w), lambda bi, i: (bi, i + n_ctx // tm, 0))
    full = lambda a: pl.BlockSpec(a.shape, lambda bi, i: (0,) * a.ndim)
    tab = pl.BlockSpec((tm, LANES), lambda bi, i: (i, 0))
    anyspec = pl.BlockSpec(memory_space=pl.ANY)
    outs = [jax.ShapeDtypeStruct((b, n, w), BF16) for w in (FOURIER_WIDTH, FOURIER_WIDTH, DIFF_QK_WIDTH)]
    outs += [jax.ShapeDtypeStruct(k_all.shape, BF16), jax.ShapeDtypeStruct(v_all.shape, BF16)]
    return pl.pallas_call(
        _prep_latent_kernel,
        out_shape=outs,
        grid=(b, n // tm),
        in_specs=[tok(d), pl.BlockSpec((1, N_MOD, d), lambda bi, i: (bi, 0, 0)), full(g), full(w_in_b),
                  tab, tab, tab, full(cc), full(sc), full(wf), anyspec, anyspec],
        out_specs=[tok(FOURIER_WIDTH), tok(FOURIER_WIDTH), tok(DIFF_QK_WIDTH), behind(DIFF_QK_WIDTH), behind(DIFF_V_WIDTH)],
        input_output_aliases={10: 3, 11: 4},
        compiler_params=_cparams(("parallel", "parallel")),
        name="prep_latent",
    )(x, mod, g, w_in_b, cos, sn, sp, cc, sc, wf, k_all, v_all)


def _prep_ctx_kernel(x_ref, mod_ref, g_ref, wk_ref, wv_ref, k_ref, v_ref):
    x = x_ref[0]
    h = _rms(x) * g_ref[...]
    h = h * (1.0 + mod_ref[0, 1:2, :]) + mod_ref[0, 0:1, :]
    hb = h.astype(BF16)
    k_ref[0] = jnp.dot(hb, wk_ref[...], preferred_element_type=F32).astype(BF16)
    v_ref[0] = jnp.dot(hb, wv_ref[...], preferred_element_type=F32).astype(BF16)


def _prep_ctx(ctx, mod, g, wk, wv, ctx_row, n_lat):
    b, n, d = ctx.shape
    tok = lambda w: pl.BlockSpec((1, n, w), lambda bi: (bi, 0, 0))
    full = lambda a: pl.BlockSpec(a.shape, lambda bi: (0,) * a.ndim)
    return pl.pallas_call(
        _prep_ctx_kernel,
        out_shape=[jax.ShapeDtypeStruct((b, n + n_lat, DIFF_QK_WIDTH), BF16),
                   jax.ShapeDtypeStruct((b, n + n_lat, DIFF_V_WIDTH), BF16)],
        grid=(b,),
        in_specs=[tok(d), pl.BlockSpec((1, N_MOD, d), lambda bi: (ctx_row, 0, 0)), full(g), full(wk), full(wv)],
        out_specs=[tok(DIFF_QK_WIDTH), tok(DIFF_V_WIDTH)],
        compiler_params=_cparams(("parallel",)),
        name="prep_ctx",
    )(ctx, mod, g, wk, wv)


def _dft1_kernel(u_ref, v_ref, mu_ref, mv_ref, yr_ref, yi_ref):
    n2 = u_ref.shape[1]
    y = (jnp.dot(mu_ref[...], u_ref[0], preferred_element_type=F32)
         + jnp.dot(mv_ref[...], v_ref[0], preferred_element_type=F32))
    yr_ref[0] = y[:n2].astype(BF16)
    yi_ref[0] = y[n2:].astype(BF16)


def _dft2_kernel(yr_ref, yi_ref, a_ref, b_ref, o_ref):
    y = (jnp.dot(a_ref[0], yr_ref[0], preferred_element_type=F32)
         + jnp.dot(b_ref[0], yi_ref[0], preferred_element_type=F32))
    o_ref[0] = y.reshape(DFT_N1, DFT_KGROUP, FOURIER_WIDTH).astype(BF16)


def _fourier(u, v, m_u, m_v, a2, b2):
    b, n, w = u.shape
    n2 = n // DFT_N1
    cols = DFT_N1 * w
    tn = min(cols, 2048)
    u2 = u.reshape(b, n2, cols)
    v2 = v.reshape(b, n2, cols)
    blk = pl.BlockSpec((1, n2, tn), lambda bi, j: (bi, 0, j))
    mat = pl.BlockSpec((2 * n2, n2), lambda bi, j: (0, 0))
    yr, yi = pl.pallas_call(
        _dft1_kernel,
        out_shape=[jax.ShapeDtypeStruct((b, n2, cols), BF16)] * 2,
        grid=(b, cols // tn),
        in_specs=[blk, blk, mat, mat],
        out_specs=[blk, blk],
        compiler_params=_cparams(("parallel", "parallel")),
        name="dft_stage1",
    )(u2, v2, m_u, m_v)
    yr = yr.reshape(b, n, w)
    yi = yi.reshape(b, n, w)
    rows = DFT_N1 * DFT_KGROUP
    groups = n2 // DFT_KGROUP
    yblk = pl.BlockSpec((1, rows, w), lambda bi, gi: (bi, gi, 0))
    mblk = pl.BlockSpec((1, rows, rows), lambda bi, gi: (gi, 0, 0))
    out = pl.pallas_call(
        _dft2_kernel,
        out_shape=jax.ShapeDtypeStruct((b, DFT_N1, n2, w), BF16),
        grid=(b, groups),
        in_specs=[yblk, yblk, mblk, mblk],
        out_specs=pl.BlockSpec((1, DFT_N1, DFT_KGROUP, w), lambda bi, gi: (bi, 0, gi, 0)),
        compiler_params=_cparams(("parallel", "parallel")),
        name="dft_stage2",
    )(yr, yi, a2, b2)
    return out.reshape(b, n, w)


def _attn_kernel(q_ref, k_ref, v_ref, lam_ref, g_ref, o_ref,
                 qs_ref, m_ref, l_ref, acc_ref, *, tk):
    tq = q_ref.shape[1]
    n_keys = k_ref.shape[1]
    q = q_ref[0]
    lane = lax.broadcasted_iota(jnp.int32, (tq, LANES), 1)
    zero = jnp.zeros_like(q)
    qs_ref[0:tq, :] = jnp.where(lane < DIFF_HEAD_DIM, q, zero)
    qs_ref[tq:, :] = jnp.where(lane >= DIFF_HEAD_DIM, q, zero)
    m_ref[...] = jnp.full(m_ref.shape, -jnp.inf, F32)
    l_ref[...] = jnp.zeros(l_ref.shape, F32)
    acc_ref[...] = jnp.zeros(acc_ref.shape, F32)

    def chunk(k, v):
        s = lax.dot_general(qs_ref[...], k, (((1,), (1,)), ((), ())), preferred_element_type=F32)
        cols = [s[:, c * LANES:(c + 1) * LANES] for c in range(k.shape[0] // LANES)]
        mx = cols[0]
        for col in cols[1:]:
            mx = jnp.maximum(mx, col)
        m_prev = m_ref[...]
        m_new = jnp.maximum(m_prev, jnp.max(mx, axis=1, keepdims=True))
        alpha = jnp.exp(m_prev - m_new)
        ps = [jnp.exp(col - m_new) for col in cols]
        psum = ps[0]
        for p in ps[1:]:
            psum = psum + p
        l_ref[...] = alpha * l_ref[...] + psum
        p = jnp.concatenate([p.astype(BF16) for p in ps], axis=1)
        acc_ref[...] = alpha * acc_ref[...] + jnp.dot(p, v, preferred_element_type=F32)
        m_ref[...] = m_new

    n_chunks = n_keys // tk
    if n_chunks <= MAX_UNROLLED_KV_CHUNKS:
        for j in range(n_chunks):
            chunk(k_ref[0, j * tk:(j + 1) * tk, :], v_ref[0, j * tk:(j + 1) * tk, :])
    else:
        def body(j, carry):
            start = pl.multiple_of(j * tk, tk)
            chunk(k_ref[0, pl.ds(start, tk), :], v_ref[0, pl.ds(start, tk), :])
            return carry

        lax.fori_loop(0, n_chunks, body, 0)

    lam = (jnp.exp(jnp.sum(lam_ref[0:1, :] * lam_ref[1:2, :], axis=-1, keepdims=True))
           - jnp.exp(jnp.sum(lam_ref[2:3, :] * lam_ref[3:4, :], axis=-1, keepdims=True)) + LAM_INIT)
    l = jnp.sum(l_ref[...], axis=1, keepdims=True)
    o = acc_ref[0:tq, :] / l[0:tq] - lam * (acc_ref[tq:, :] / l[tq:])
    o_ref[0] =(_rms(o) * g_ref[...] * (1.0 - LAM_INIT)).astype(BF16)


def _attention(q, k_all, v_all, lam_vecs, g_subln, tq, tk):
    b, n, _ = q.shape
    n_keys = k_all.shape[1]
    assert n_keys % tk == 0
    qspec = pl.BlockSpec((1, tq, LANES), lambda bi, h, i: (bi, i, h))
    kspec = pl.BlockSpec((1, n_keys, LANES), lambda bi, h, i: (bi, 0, h))
    return pl.pallas_call(
        functools.partial(_attn_kernel, tk=tk),
        out_shape=jax.ShapeDtypeStruct((b, n, DIFF_V_WIDTH), BF16),
        grid=(b, DIFF_HEADS, n // tq),
        in_specs=[qspec, kspec, kspec,
                  pl.BlockSpec(lam_vecs.shape, lambda bi, h, i: (0, 0)),
                  pl.BlockSpec(g_subln.shape, lambda bi, h, i: (0, 0))],
        out_specs=qspec,
        scratch_shapes=[pltpu.VMEM((2 * tq, LANES), BF16), pltpu.VMEM((2 * tq, LANES), F32),
                        pltpu.VMEM((2 * tq, LANES), F32), pltpu.VMEM((2 * tq, LANES), F32)],
        compiler_params=_cparams(("parallel", "parallel", "arbitrary")),
        name="diff_attention",
    )(q, k_all, v_all, lam_vecs, g_subln)


def _route(logits_t, bias_col):
    tm = logits_t.shape[1]
    scores = jax.nn.sigmoid(logits_t)
    sel = scores + bias_col
    ninf = jnp.float32(-jnp.inf)
    i8 = lax.broadcasted_iota(jnp.int32, (GROUP_SIZE, tm), 0)
    slabs = [sel[GROUP_SIZE * g:GROUP_SIZE * (g + 1)] for g in range(N_EXPERT_GROUPS)]
    rows = []
    for slab in slabs:
        m1 = jnp.max(slab, axis=0, keepdims=True)
        i1 = jnp.min(jnp.where(slab == m1, i8, GROUP_SIZE), axis=0, keepdims=True)
        m2 = jnp.max(jnp.where(i8 == i1, ninf, slab), axis=0, keepdims=True)
        rows.append(m1 + m2)
    gs = jnp.concatenate(rows, axis=0)
    gmask = jnp.zeros(gs.shape, jnp.bool_)
    for _ in range(TOP_K_GROUPS):
        m = jnp.max(gs, axis=0, keepdims=True)
        i = jnp.min(jnp.where(gs == m, i8, N_EXPERT_GROUPS), axis=0, keepdims=True)
        hit = i8 == i
        gmask = jnp.logical_or(gmask, hit)
        gs = jnp.where(hit, ninf, gs)
    gmask_f = gmask.astype(F32)
    cand = [jnp.where(gmask_f[g:g + 1] > 0.5, slabs[g], ninf) for g in range(N_EXPERT_GROUPS)]
    chosen = [jnp.zeros((GROUP_SIZE, tm), jnp.bool_) for _ in range(N_EXPERT_GROUPS)]
    for _ in range(TOP_K):
        mm = cand[0]
        for g in range(1, N_EXPERT_GROUPS):
            mm = jnp.maximum(mm, cand[g])
        m = jnp.max(mm, axis=0, keepdims=True)
        ii = jnp.where(cand[0] == m, i8, N_EXPERTS)
        for g in range(1, N_EXPERT_GROUPS):
            ii = jnp.minimum(ii, jnp.where(cand[g] == m, i8 + GROUP_SIZE * g, N_EXPERTS))
        idx = jnp.min(ii, axis=0, keepdims=True)
        for g in range(N_EXPERT_GROUPS):
            hit = (i8 + GROUP_SIZE * g) == idx
            chosen[g] = jnp.logical_or(chosen[g], hit)
            cand[g] = jnp.where(hit, ninf, cand[g])
    picked = [jnp.where(chosen[g], scores[GROUP_SIZE * g:GROUP_SIZE * (g + 1)], 0.0) for g in range(N_EXPERT_GROUPS)]
    tot = picked[0]
    for g in range(1, N_EXPERT_GROUPS):
        tot = tot + picked[g]
    denom = jnp.sum(tot, axis=0, keepdims=True) + 1e-20
    return jnp.concatenate([p / denom * ROUTED_SCALE for p in picked], axis=0)


def _mix_kernel(four_ref, att_ref, x_ref, mod_ref, gpost_ref, gpre_ref, wo_ref, wr_ref, rb_ref,
                wsg_ref, wsu_ref, wsd_ref, x1_ref, h2_ref, sh_ref, gate_ref):
    y = (jnp.dot(four_ref[0], wo_ref[0:FOURIER_WIDTH, :], preferred_element_type=F32)
         + jnp.dot(att_ref[0], wo_ref[FOURIER_WIDTH:, :], preferred_element_type=F32))
    x1 = x_ref[0] + mod_ref[0, 2:3, :] * (_rms(y) * gpost_ref[...])
    x1_ref[0] = x1
    h2 = _rms(x1) * gpre_ref[...]
    h2 = h2 * (1.0 + mod_ref[0, 4:5, :]) + mod_ref[0, 3:4, :]
    hb = h2.astype(BF16)
    h2_ref[0] = hb
    a = jnp.dot(hb, wsg_ref[...], preferred_element_type=F32)
    u = jnp.dot(hb, wsu_ref[...], preferred_element_type=F32)
    hid = (a * jax.nn.sigmoid(a)) * u
    sh_ref[0] = jnp.dot(hid.astype(BF16), wsd_ref[...], preferred_element_type=F32)
    logits = jnp.dot(h2, wr_ref[...], preferred_element_type=F32, precision=lax.Precision.HIGHEST)
    gate_ref[0] = _route(logits.T, rb_ref[...]).T


def _mix(four, att, x, mod, g_post, g_pre, w_out_b, w_router, rb_col, wsg, wsu, wsd, tm):
    b, n, d = x.shape
    tok = lambda w: pl.BlockSpec((1, tm, w), lambda bi, i: (bi, i, 0))
    full = lambda a: pl.BlockSpec(a.shape, lambda bi, i: (0,) * a.ndim)
    return pl.pallas_call(
        _mix_kernel,
        out_shape=[jax.ShapeDtypeStruct((b, n, d), F32), jax.ShapeDtypeStruct((b, n, d), BF16),
                   jax.ShapeDtypeStruct((b, n, d), F32), jax.ShapeDtypeStruct((b, n, N_EXPERTS), F32)],
        grid=(b, n // tm),
        in_specs=[tok(FOURIER_WIDTH), tok(DIFF_V_WIDTH), tok(d), pl.BlockSpec((1, N_MOD, d), lambda bi, i: (bi, 0, 0)),
                  full(g_post), full(g_pre), full(w_out_b), full(w_router), full(rb_col), full(wsg), full(wsu), full(wsd)],
        out_specs=[tok(d), tok(d), tok(d), tok(N_EXPERTS)],
        compiler_params=_cparams(("parallel", "parallel")),
        name="mix",
    )(four, att, x, mod, g_post, g_pre, w_out_b, w_router, rb_col, wsg, wsu, wsd)


def _moe_kernel(h_ref, gate_ref, sh_ref, x1_ref, mod_ref, g_ref, wg_ref, wu_ref, wd_ref, o_ref, acc_ref):
    e = pl.program_id(2)

    @pl.when(e == 0)
    def _():
        acc_ref[...] = sh_ref[0]

    h = h_ref[0]
    a = jnp.dot(h, wg_ref[0], preferred_element_type=F32)
    u = jnp.dot(h, wu_ref[0], preferred_element_type=F32)
    lane = lax.broadcasted_iota(jnp.int32, gate_ref.shape[1:], 1)
    gcol = jnp.sum(jnp.where(lane == e, gate_ref[0], 0.0), axis=-1, keepdims=True)
    hid = (a * jax.nn.sigmoid(a)) * u * gcol
    acc_ref[...] += jnp.dot(hid.astype(BF16), wd_ref[0], preferred_element_type=F32)

    @pl.when(e == N_EXPERTS - 1)
    def _():
        o_ref[0] = x1_ref[0] + mod_ref[0, 5:6, :] * (_rms(acc_ref[...]) * g_ref[...])


def _moe(h2, gate, shared, x1, mod, g_post, wg, wu, wd, tm):
    b, n, d = x1.shape
    de = wg.shape[-1]
    tok = lambda w: pl.BlockSpec((1, tm, w), lambda bi, i, e: (bi, i, 0))
    return pl.pallas_call(
        _moe_kernel,
        out_shape=jax.ShapeDtypeStruct((b, n, d), F32),
        grid=(b, n // tm, N_EXPERTS),
        in_specs=[tok(d), tok(N_EXPERTS), tok(d), tok(d), pl.BlockSpec((1, N_MOD, d), lambda bi, i, e: (bi, 0, 0)),
                  pl.BlockSpec(g_post.shape, lambda bi, i, e: (0, 0)),
                  pl.BlockSpec((1, d, de), lambda bi, i, e: (e, 0, 0)),
                  pl.BlockSpec((1, d, de), lambda bi, i, e: (e, 0, 0)),
                  pl.BlockSpec((1, de, d), lambda bi, i, e: (e, 0, 0))],
        out_specs=tok(d),
        scratch_shapes=[pltpu.VMEM((tm, d), F32)],
        compiler_params=_cparams(("parallel", "parallel", "arbitrary")),
        name="moe_dense",
    )(h2, gate, shared, x1, mod, g_post, wg, wu, wd)


def kernel(x, c, ctx, c_ctx, w_mod, b_mod, g_attn_pre, g_attn_post, w_in, w_fourier, lambda_q1, lambda_k1,
           lambda_q2, lambda_k2, g_subln, w_out, g_ffn_pre, g_ffn_post, w_router, router_bias, w_gate, w_up,
           w_down, ws_gate, ws_up, ws_down):
    b, n, d = x.shape
    assert w_mod.shape[0] == 1, "single-layer kernel"
    assert n % (DFT_N1 * DFT_KGROUP) == 0 and b + 1 <= 8
    row = lambda a: a[0].reshape(1, -1)

    cond8 = jnp.zeros((8, d), F32).at[:b].set(c).at[b].set(c_ctx)
    mod = _ada(cond8, w_mod[0], b_mod[0]).reshape(8, N_MOD, d)

    w_in_b = w_in[0].astype(BF16)
    o2 = FOURIER_WIDTH + DIFF_QK_WIDTH
    o3 = o2 + DIFF_QK_WIDTH
    cc, sc = _channel_dft(n)
    wf = jnp.zeros((FOURIER_GROUPS, FOURIER_GROUP_DIM, FOURIER_GROUPS, FOURIER_GROUP_DIM), F32)
    for g in range(FOURIER_GROUPS):
        wf = wf.at[g, :, g, :].set(w_fourier[0, g])
    wf = wf.reshape(FOURIER_WIDTH, FOURIER_WIDTH).astype(BF16)

    tiles = _tiles(n, ctx.shape[1])
    k_all, v_all = _prep_ctx(ctx, mod, row(g_attn_pre), w_in_b[:, o2:o3], w_in_b[:, o3:], b, n)
    u, v2, q, k_all, v_all = _prep_latent(x, mod, row(g_attn_pre), w_in_b, _rope_tables(n), cc, sc, wf,
                                          k_all, v_all, tiles["prep"])

    m_u, m_v = _dft_stage1(n // DFT_N1)
    a2, b2 = _dft_stage2(n)
    four = _fourier(u, v2, m_u, m_v, a2, b2)

    lam_vecs = jnp.concatenate([lambda_q1, lambda_k1, lambda_q2, lambda_k2], axis=0).astype(F32)
    att = _attention(q, k_all, v_all, lam_vecs, row(g_subln), tiles["attn_q"], tiles["attn_kv"])

    x1, h2, shared, gate = _mix(four, att, x, mod, row(g_attn_post), row(g_ffn_pre), w_out[0].astype(BF16),
                                w_router[0], router_bias[0].reshape(-1, 1), ws_gate[0].astype(BF16),
                                ws_up[0].astype(BF16), ws_down[0].astype(BF16), tiles["mix"])
    return _moe(h2, gate, shared, x1, mod, row(g_ffn_post), w_gate[0].astype(BF16), w_up[0].astype(BF16),
                w_down[0].astype(BF16), tiles["moe"])
```

```python
import functools
import math

import numpy as np
import jax
import jax.numpy as jnp
from jax import lax
from jax.experimental import pallas as pl
from jax.experimental.pallas import tpu as pltpu

GRID_W = 64
N_MOD = 6
NORM_EPS = 1e-6
FOURIER_GROUPS = 4
FOURIER_GROUP_DIM = 64
FOURIER_WIDTH = FOURIER_GROUPS * FOURIER_GROUP_DIM
DIFF_HEADS = 6
DIFF_HEAD_DIM = 64
DIFF_VALUE_DIM = 2 * DIFF_HEAD_DIM
DIFF_QK_WIDTH = DIFF_HEADS * 2 * DIFF_HEAD_DIM
DIFF_V_WIDTH = DIFF_HEADS * DIFF_VALUE_DIM
ROPE_BASE = 10000.0
ROPE_AXIS_DIM = DIFF_HEAD_DIM // 2
N_EXPERTS = 64
TOP_K = 8
N_EXPERT_GROUPS = 8
TOP_K_GROUPS = 4
GROUP_SIZE = N_EXPERTS // N_EXPERT_GROUPS
ROUTED_SCALE = 2.5
LAM_INIT = 0.8 - 0.6 * math.exp(-0.3 * 0)
Q_SCALE = DIFF_HEAD_DIM ** -0.5

DFT_N1 = 64
DFT_KGROUP = 8
LANES = 128
VMEM_LIMIT = 56 * 1024 * 1024

BF16 = jnp.bfloat16
F32 = jnp.float32


MOE_TILE = 256
SEG = 16
MOE_BLOCK = 512
TAIL_SLOTS = MOE_BLOCK // SEG - 1
LOC_BLOCK = 512
LOC_ROWS = -(-(MOE_TILE * TOP_K + N_EXPERTS * (SEG - 1)) // LOC_BLOCK) * LOC_BLOCK
ALWAYS_USED_BLOCKS = MOE_TILE * TOP_K // LOC_BLOCK

MAX_UNROLLED_KV_CHUNKS = 4
ATTN_KV_TARGET = 2816


def _tiles(n, n_ctx):
    n_keys = n + n_ctx
    kv = max(t for t in range(LANES, n_keys + 1, LANES) if n_keys % t == 0 and t <= ATTN_KV_TARGET)
    return {"attn_q": min(n, 512), "attn_kv": kv}


def _cparams(sem):
    return pltpu.CompilerParams(dimension_semantics=sem, vmem_limit_bytes=VMEM_LIMIT)


def _bdot(a, b):
    return jnp.dot(a.astype(BF16), b.astype(BF16), preferred_element_type=F32)


def _rms(x):
    return x * lax.rsqrt(jnp.mean(x * x, axis=-1, keepdims=True) + NORM_EPS)


_TWO_PI_HI = float(np.float32(2.0 * np.pi))
_TWO_PI_LO = float(np.float32(2.0 * np.pi - _TWO_PI_HI))


def _cos_sin(phase, period):
    t = phase.astype(F32) / period
    ang = t * _TWO_PI_HI + t * _TWO_PI_LO
    return jnp.cos(ang), jnp.sin(ang)


def _rope_tables(n_tokens):
    rows = n_tokens // GRID_W
    row = jnp.broadcast_to(jnp.arange(rows)[:, None], (rows, GRID_W)).reshape(-1)
    col = jnp.broadcast_to(jnp.arange(GRID_W)[None, :], (rows, GRID_W)).reshape(-1)
    inv_freq = ROPE_BASE ** (-jnp.arange(0, ROPE_AXIS_DIM, 2, dtype=F32) / ROPE_AXIS_DIM)
    pos = jnp.stack([row, col], axis=-1).astype(F32)
    ang = pos[:, :, None] * inv_freq
    cos, sin = jnp.cos(ang), jnp.sin(ang)
    z = jnp.zeros_like(sin[:, 0])
    reps = LANES // DIFF_HEAD_DIM
    cos_l = jnp.tile(jnp.concatenate([cos[:, 0], cos[:, 0], cos[:, 1], cos[:, 1]], axis=-1), (1, reps))
    s_next = jnp.tile(jnp.concatenate([-sin[:, 0], z, -sin[:, 1], z], axis=-1), (1, reps))
    s_prev = jnp.tile(jnp.concatenate([z, sin[:, 0], z, sin[:, 1]], axis=-1), (1, reps))
    return cos_l, s_next, s_prev


def _channel_dft(n_tokens):
    c = jnp.arange(FOURIER_GROUP_DIM, dtype=jnp.int32)
    cs, sn = _cos_sin((c[:, None] * c[None, :]) % FOURIER_GROUP_DIM, FOURIER_GROUP_DIM)
    norm = lax.rsqrt(jnp.float32(n_tokens * FOURIER_GROUP_DIM))
    eye = jnp.eye(FOURIER_GROUPS, dtype=F32)
    return jnp.kron(eye, cs * norm).astype(BF16), jnp.kron(eye, sn * norm).astype(BF16)


def _dft_stage1(n2):
    k = jnp.arange(n2, dtype=jnp.int32)
    c, s = _cos_sin((k[:, None] * k[None, :]) % n2, n2)
    m_u = jnp.concatenate([c, -s], axis=0)
    m_v = jnp.concatenate([-s, -c], axis=0)
    return m_u.astype(BF16), m_v.astype(BF16)


def _dft_stage2(n_tokens):
    n2 = n_tokens // DFT_N1
    groups = n2 // DFT_KGROUP
    k1 = jnp.arange(DFT_N1, dtype=jnp.int32)[None, :, None, None]
    n1 = jnp.arange(DFT_N1, dtype=jnp.int32)[None, None, None, :]
    k2 = (jnp.arange(groups, dtype=jnp.int32)[:, None, None, None] * DFT_KGROUP
          + jnp.arange(DFT_KGROUP, dtype=jnp.int32)[None, None, :, None])
    c, s = _cos_sin((k1 * n1 * n2 + k2 * n1) % n_tokens, n_tokens)
    eye = jnp.eye(DFT_KGROUP, dtype=F32)[None, None, :, :, None]
    r = DFT_N1 * DFT_KGROUP
    a = (c[:, :, :, None, :] * eye).reshape(groups, r, r)
    b = (s[:, :, :, None, :] * eye).reshape(groups, r, r)
    return a.astype(BF16), b.astype(BF16)


def _ada_kernel(c_ref, w_ref, b_ref, o_ref):
    c = c_ref[...]
    s = c * jax.nn.sigmoid(c)
    o_ref[...] = jnp.dot(s, w_ref[...], preferred_element_type=F32) + b_ref[...]


def _ada(cond8, w_mod, b_mod):
    d = cond8.shape[1]
    return pl.pallas_call(
        _ada_kernel,
        out_shape=jax.ShapeDtypeStruct((8, N_MOD * d), F32),
        grid=(N_MOD,),
        in_specs=[pl.BlockSpec((8, d), lambda j: (0, 0)),
                  pl.BlockSpec((d, d), lambda j: (0, j)),
                  pl.BlockSpec((1, d), lambda j: (0, j))],
        out_specs=pl.BlockSpec((8, d), lambda j: (0, j)),
        compiler_params=_cparams(("arbitrary",)),
        name="ada",
    )(cond8, w_mod, b_mod.reshape(1, -1))


def _rope_chunk(ch, cos, s_next, s_prev):
    nxt = pltpu.roll(ch, LANES - ROPE_AXIS_DIM // 2, 1)
    prv = pltpu.roll(ch, ROPE_AXIS_DIM // 2, 1)
    return ch * cos + nxt * s_next + prv * s_prev


def _modulated(x, g_ref, mod_ref):
    h = _rms(x) * g_ref[...]
    return (h * (1.0 + mod_ref[0, 1:2, :]) + mod_ref[0, 0:1, :]).astype(BF16)


def _prep_kernel(x_ref, ctx_ref, mod_ref, modc_ref, g_ref, w_ref, cos_ref, sn_ref, sp_ref, cc_ref, sc_ref, wf_ref,
                 u_ref, v2_ref, q_ref, k_ref, v_ref):
    o1 = FOURIER_WIDTH
    o2 = o1 + DIFF_QK_WIDTH
    o3 = o2 + DIFF_QK_WIDTH
    step = pl.program_id(1)

    @pl.when(step == 0)
    def _():
        hb = _modulated(ctx_ref[0], g_ref, modc_ref)
        k_ref[0] = jnp.dot(hb, w_ref[:, o2:o3], preferred_element_type=F32).astype(BF16)
        v_ref[0] = jnp.dot(hb, w_ref[:, o3:], preferred_element_type=F32).astype(BF16)

    @pl.when(step > 0)
    def _():
        hb = _modulated(x_ref[0], g_ref, mod_ref)
        f = jnp.dot(hb, w_ref[:, 0:o1], preferred_element_type=F32).astype(BF16)
        fc = jnp.dot(f, cc_ref[...], preferred_element_type=F32).astype(BF16)
        fs = jnp.dot(f, sc_ref[...], preferred_element_type=F32).astype(BF16)
        u_ref[0] = jnp.dot(fc, wf_ref[...], preferred_element_type=F32).astype(BF16)
        v2_ref[0] = jnp.dot(fs, wf_ref[...], preferred_element_type=F32).astype(BF16)
        cos, sn, sp = cos_ref[...], sn_ref[...], sp_ref[...]
        for j in range(DIFF_QK_WIDTH // LANES):
            lo = j * LANES
            qc = jnp.dot(hb, w_ref[:, o1 + lo:o1 + lo + LANES], preferred_element_type=F32)
            q_ref[0, :, lo:lo + LANES] = (_rope_chunk(qc, cos, sn, sp) * Q_SCALE).astype(BF16)
            kc = jnp.dot(hb, w_ref[:, o2 + lo:o2 + lo + LANES], preferred_element_type=F32)
            k_ref[0, :, lo:lo + LANES] = _rope_chunk(kc, cos, sn, sp).astype(BF16)
        v_ref[0] = jnp.dot(hb, w_ref[:, o3:], preferred_element_type=F32).astype(BF16)


def _prep(x, ctx, mod, ctx_row, g, w_in_b, rope, cc, sc, wf):
    b, n, d = x.shape
    tm = ctx.shape[1]
    assert n % tm == 0
    cos, sn, sp = rope
    lat = lambda i: jnp.maximum(i - 1, 0)
    tok = lambda w: pl.BlockSpec((1, tm, w), lambda bi, i: (bi, lat(i), 0))
    allk = lambda w: pl.BlockSpec((1, tm, w), lambda bi, i: (bi, i, 0))
    full = lambda a: pl.BlockSpec(a.shape, lambda bi, i: (0,) * a.ndim)
    tab = pl.BlockSpec((tm, LANES), lambda bi, i: (lat(i), 0))
    outs = [jax.ShapeDtypeStruct((b, n, w), BF16) for w in (FOURIER_WIDTH, FOURIER_WIDTH, DIFF_QK_WIDTH)]
    outs += [jax.ShapeDtypeStruct((b, tm + n, w), BF16) for w in (DIFF_QK_WIDTH, DIFF_V_WIDTH)]
    return pl.pallas_call(
        _prep_kernel,
        out_shape=outs,
        grid=(b, 1 + n // tm),
        in_specs=[tok(d), pl.BlockSpec((1, tm, d), lambda bi, i: (bi, 0, 0)),
                  pl.BlockSpec((1, N_MOD, d), lambda bi, i: (bi, 0, 0)),
                  pl.BlockSpec((1, N_MOD, d), lambda bi, i: (ctx_row, 0, 0)),
                  full(g), full(w_in_b), tab, tab, tab, full(cc), full(sc), full(wf)],
        out_specs=[tok(FOURIER_WIDTH), tok(FOURIER_WIDTH), tok(DIFF_QK_WIDTH), allk(DIFF_QK_WIDTH), allk(DIFF_V_WIDTH)],
        compiler_params=_cparams(("parallel", "arbitrary")),
        name="prep",
    )(x, ctx, mod, mod, g, w_in_b, cos, sn, sp, cc, sc, wf)


def _dft1_kernel(u_ref, v_ref, mu_ref, mv_ref, yr_ref, yi_ref):
    n2 = u_ref.shape[1]
    y = (jnp.dot(mu_ref[...], u_ref[0], preferred_element_type=F32)
         + jnp.dot(mv_ref[...], v_ref[0], preferred_element_type=F32))
    yr_ref[0] = y[:n2].astype(BF16)
    yi_ref[0] = y[n2:].astype(BF16)


def _dft2_kernel(yr_ref, yi_ref, a_ref, b_ref, o_ref):
    y = (jnp.dot(a_ref[0], yr_ref[0], preferred_element_type=F32)
         + jnp.dot(b_ref[0], yi_ref[0], preferred_element_type=F32))
    o_ref[0] = y.reshape(DFT_N1, DFT_KGROUP, FOURIER_WIDTH).astype(BF16)


def _fourier(u, v, m_u, m_v, a2, b2):
    b, n, w = u.shape
    n2 = n // DFT_N1
    cols = DFT_N1 * w
    tn = min(cols, 2048)
    u2 = u.reshape(b, n2, cols)
    v2 = v.reshape(b, n2, cols)
    blk = pl.BlockSpec((1, n2, tn), lambda bi, j: (bi, 0, j))
    mat = pl.BlockSpec((2 * n2, n2), lambda bi, j: (0, 0))
    yr, yi = pl.pallas_call(
        _dft1_kernel,
        out_shape=[jax.ShapeDtypeStruct((b, n2, cols), BF16)] * 2,
        grid=(b, cols // tn),
        in_specs=[blk, blk, mat, mat],
        out_specs=[blk, blk],
        compiler_params=_cparams(("parallel", "parallel")),
        name="dft_stage1",
    )(u2, v2, m_u, m_v)
    yr = yr.reshape(b, n, w)
    yi = yi.reshape(b, n, w)
    rows = DFT_N1 * DFT_KGROUP
    groups = n2 // DFT_KGROUP
    yblk = pl.BlockSpec((1, rows, w), lambda bi, gi: (bi, gi, 0))
    mblk = pl.BlockSpec((1, rows, rows), lambda bi, gi: (gi, 0, 0))
    out = pl.pallas_call(
        _dft2_kernel,
        out_shape=jax.ShapeDtypeStruct((b, DFT_N1, n2, w), BF16),
        grid=(b, groups),
        in_specs=[yblk, yblk, mblk, mblk],
        out_specs=pl.BlockSpec((1, DFT_N1, DFT_KGROUP, w), lambda bi, gi: (bi, 0, gi, 0)),
        compiler_params=_cparams(("parallel", "parallel")),
        name="dft_stage2",
    )(yr, yi, a2, b2)
    return out.reshape(b, n, w)


def _attn_kernel(q_ref, k_ref, v_ref, lam_ref, g_ref, o_ref,
                 qs_ref, m_ref, l_ref, acc_ref, *, tk):
    tq = q_ref.shape[1]
    n_keys = k_ref.shape[1]
    q = q_ref[0]
    lane = lax.broadcasted_iota(jnp.int32, (tq, LANES), 1)
    zero = jnp.zeros_like(q)
    qs_ref[0:tq, :] = jnp.where(lane < DIFF_HEAD_DIM, q, zero)
    qs_ref[tq:, :] = jnp.where(lane >= DIFF_HEAD_DIM, q, zero)
    m_ref[...] = jnp.full(m_ref.shape, -jnp.inf, F32)
    l_ref[...] = jnp.zeros(l_ref.shape, F32)
    acc_ref[...] = jnp.zeros(acc_ref.shape, F32)

    def chunk(k, v):
        s = lax.dot_general(qs_ref[...], k, (((1,), (1,)), ((), ())), preferred_element_type=F32)
        cols = [s[:, c * LANES:(c + 1) * LANES] for c in range(k.shape[0] // LANES)]
        mx = cols[0]
        for col in cols[1:]:
            mx = jnp.maximum(mx, col)
        m_prev = m_ref[...]
        m_new = jnp.maximum(m_prev, jnp.max(mx, axis=1, keepdims=True))
        alpha = jnp.exp(m_prev - m_new)
        ps = [jnp.exp(col - m_new) for col in cols]
        psum = ps[0]
        for p in ps[1:]:
            psum = psum + p
        l_ref[...] = alpha * l_ref[...] + psum
        p = jnp.concatenate([p.astype(BF16) for p in ps], axis=1)
        acc_ref[...] = alpha * acc_ref[...] + jnp.dot(p, v, preferred_element_type=F32)
        m_ref[...] = m_new

    n_chunks = n_keys // tk
    if n_chunks <= MAX_UNROLLED_KV_CHUNKS:
        for j in range(n_chunks):
            chunk(k_ref[0, j * tk:(j + 1) * tk, :], v_ref[0, j * tk:(j + 1) * tk, :])
    else:
        def body(j, carry):
            start = pl.multiple_of(j * tk, tk)
            chunk(k_ref[0, pl.ds(start, tk), :], v_ref[0, pl.ds(start, tk), :])
            return carry

        lax.fori_loop(0, n_chunks, body, 0)

    lam = (jnp.exp(jnp.sum(lam_ref[0:1, :] * lam_ref[1:2, :], axis=-1, keepdims=True))
           - jnp.exp(jnp.sum(lam_ref[2:3, :] * lam_ref[3:4, :], axis=-1, keepdims=True)) + LAM_INIT)
    l = jnp.sum(l_ref[...], axis=1, keepdims=True)
    o = acc_ref[0:tq, :] / l[0:tq] - lam * (acc_ref[tq:, :] / l[tq:])
    o_ref[0] =(_rms(o) * g_ref[...] * (1.0 - LAM_INIT)).astype(BF16)


def _attention(q, k_all, v_all, lam_vecs, g_subln, tq, tk):
    b, n, _ = q.shape
    n_keys = k_all.shape[1]
    assert n_keys % tk == 0
    qspec = pl.BlockSpec((1, tq, LANES), lambda bi, h, i: (bi, i, h))
    kspec = pl.BlockSpec((1, n_keys, LANES), lambda bi, h, i: (bi, 0, h))
    return pl.pallas_call(
        functools.partial(_attn_kernel, tk=tk),
        out_shape=jax.ShapeDtypeStruct((b, n, DIFF_V_WIDTH), BF16),
        grid=(b, DIFF_HEADS, n // tq),
        in_specs=[qspec, kspec, kspec,
                  pl.BlockSpec(lam_vecs.shape, lambda bi, h, i: (0, 0)),
                  pl.BlockSpec(g_subln.shape, lambda bi, h, i: (0, 0))],
        out_specs=qspec,
        scratch_shapes=[pltpu.VMEM((2 * tq, LANES), BF16), pltpu.VMEM((2 * tq, LANES), F32),
                        pltpu.VMEM((2 * tq, LANES), F32), pltpu.VMEM((2 * tq, LANES), F32)],
        compiler_params=_cparams(("parallel", "parallel", "arbitrary")),
        name="diff_attention",
    )(q, k_all, v_all, lam_vecs, g_subln)


def _route(logits_t, bias_col):
    tm = logits_t.shape[1]
    scores = jax.nn.sigmoid(logits_t)
    sel = scores + bias_col
    ninf = jnp.float32(-jnp.inf)
    i8 = lax.broadcasted_iota(jnp.int32, (GROUP_SIZE, tm), 0)
    slabs = [sel[GROUP_SIZE * g:GROUP_SIZE * (g + 1)] for g in range(N_EXPERT_GROUPS)]
    rows = []
    for slab in slabs:
        m1 = jnp.max(slab, axis=0, keepdims=True)
        i1 = jnp.min(jnp.where(slab == m1, i8, GROUP_SIZE), axis=0, keepdims=True)
        m2 = jnp.max(jnp.where(i8 == i1, ninf, slab), axis=0, keepdims=True)
        rows.append(m1 + m2)
    gs = jnp.concatenate(rows, axis=0)
    gmask = jnp.zeros(gs.shape, jnp.bool_)
    for _ in range(TOP_K_GROUPS):
        m = jnp.max(gs, axis=0, keepdims=True)
        i = jnp.min(jnp.where(gs == m, i8, N_EXPERT_GROUPS), axis=0, keepdims=True)
        hit = i8 == i
        gmask = jnp.logical_or(gmask, hit)
        gs = jnp.where(hit, ninf, gs)
    gmask_f = gmask.astype(F32)
    cand = [jnp.where(gmask_f[g:g + 1] > 0.5, slabs[g], ninf) for g in range(N_EXPERT_GROUPS)]
    chosen = [jnp.zeros((GROUP_SIZE, tm), jnp.bool_) for _ in range(N_EXPERT_GROUPS)]
    picks = []
    for _ in range(TOP_K):
        mm = cand[0]
        for g in range(1, N_EXPERT_GROUPS):
            mm = jnp.maximum(mm, cand[g])
        m = jnp.max(mm, axis=0, keepdims=True)
        ii = jnp.where(cand[0] == m, i8, N_EXPERTS)
        for g in range(1, N_EXPERT_GROUPS):
            ii = jnp.minimum(ii, jnp.where(cand[g] == m, i8 + GROUP_SIZE * g, N_EXPERTS))
        idx = jnp.min(ii, axis=0, keepdims=True)
        picks.append(idx)
        for g in range(N_EXPERT_GROUPS):
            hit = (i8 + GROUP_SIZE * g) == idx
            chosen[g] = jnp.logical_or(chosen[g], hit)
            cand[g] = jnp.where(hit, ninf, cand[g])
    score_slabs = [scores[GROUP_SIZE * g:GROUP_SIZE * (g + 1)] for g in range(N_EXPERT_GROUPS)]
    tot = jnp.where(chosen[0], score_slabs[0], 0.0)
    for g in range(1, N_EXPERT_GROUPS):
        tot = tot + jnp.where(chosen[g], score_slabs[g], 0.0)
    denom = jnp.sum(tot, axis=0, keepdims=True) + 1e-20

    chosen_f = jnp.concatenate([c.astype(F32) for c in chosen], axis=0)
    chosen_b = chosen_f.astype(BF16)
    earlier = (lax.broadcasted_iota(jnp.int32, (tm, tm), 0) < lax.broadcasted_iota(jnp.int32, (tm, tm), 1))
    rank = jnp.dot(chosen_b, earlier.astype(BF16), preferred_element_type=F32)
    pad = lambda cnt: jnp.floor((cnt + (SEG - 1)) * (1.0 / SEG)) * SEG
    cp_col = pad(jnp.sum(chosen_f, axis=1, keepdims=True))
    lower = (lax.broadcasted_iota(jnp.int32, (N_EXPERTS, N_EXPERTS), 1)
             < lax.broadcasted_iota(jnp.int32, (N_EXPERTS, N_EXPERTS), 0))
    seg_start = jnp.dot(lower.astype(BF16), jnp.broadcast_to(cp_col, (N_EXPERTS, LANES)).astype(BF16),
                        preferred_element_type=F32)[:, 0:1]
    base = seg_start + rank
    base_slabs = [base[GROUP_SIZE * g:GROUP_SIZE * (g + 1)] for g in range(N_EXPERT_GROUPS)]

    def at_pick(slabs_, idx):
        acc = jnp.zeros((1, tm), F32)
        for g in range(N_EXPERT_GROUPS):
            acc = acc + jnp.sum(jnp.where((i8 + GROUP_SIZE * g) == idx, slabs_[g], 0.0), axis=0, keepdims=True)
        return acc

    pos = jnp.concatenate([at_pick(base_slabs, idx) for idx in picks], axis=0).astype(jnp.int32)
    wts = jnp.concatenate([at_pick(score_slabs, idx) / denom * ROUTED_SCALE for idx in picks], axis=0)
    cnt_row = lax.dot_general(jnp.ones((GROUP_SIZE, tm), BF16), chosen_b, (((1,), (1,)), ((), ())),
                              preferred_element_type=F32)
    return pos, wts, pad(cnt_row).astype(jnp.int32)


def _mix_kernel(four_ref, att_ref, x_ref, mod_ref, gpost_ref, gpre_ref, wo_ref, wr_ref, rb_ref,
                wsg_ref, wsu_ref, wsd_ref, x1_ref, h2_ref, sh_ref, pos_ref, wts_ref, cp_ref):
    y = (jnp.dot(four_ref[0], wo_ref[0:FOURIER_WIDTH, :], preferred_element_type=F32)
         + jnp.dot(att_ref[0], wo_ref[FOURIER_WIDTH:, :], preferred_element_type=F32))
    x1 = x_ref[0] + mod_ref[0, 2:3, :] * (_rms(y) * gpost_ref[...])
    x1_ref[0] = x1
    h2 = _rms(x1) * gpre_ref[...]
    h2 = h2 * (1.0 + mod_ref[0, 4:5, :]) + mod_ref[0, 3:4, :]
    hb = h2.astype(BF16)
    h2_ref[0] = hb
    a = jnp.dot(hb, wsg_ref[...], preferred_element_type=F32)
    u = jnp.dot(hb, wsu_ref[...], preferred_element_type=F32)
    hid = (a * jax.nn.sigmoid(a)) * u
    sh_ref[0] = jnp.dot(hid.astype(BF16), wsd_ref[...], preferred_element_type=F32)
    logits = jnp.dot(h2, wr_ref[...], preferred_element_type=F32, precision=lax.Precision.HIGHEST)
    pos_ref[0], wts_ref[0], cp_ref[0] = _route(logits.T, rb_ref[...])


def _mix(four, att, x, mod, g_post, g_pre, w_out_b, w_router, rb_col, wsg, wsu, wsd):
    b, n, d = x.shape
    tm = MOE_TILE
    nt = n // tm
    tok = lambda w: pl.BlockSpec((1, tm, w), lambda bi, i: (bi, i, 0))
    full = lambda a: pl.BlockSpec(a.shape, lambda bi, i: (0,) * a.ndim)
    plan = lambda r, w: pl.BlockSpec((1, r, w), lambda bi, i: (bi * nt + i, 0, 0))
    return pl.pallas_call(
        _mix_kernel,
        out_shape=[jax.ShapeDtypeStruct((b, n, d), F32), jax.ShapeDtypeStruct((b, n, d), BF16),
                   jax.ShapeDtypeStruct((b, n, d), F32),
                   jax.ShapeDtypeStruct((b * nt, TOP_K, tm), jnp.int32), jax.ShapeDtypeStruct((b * nt, TOP_K, tm), F32),
                   jax.ShapeDtypeStruct((b * nt, GROUP_SIZE, N_EXPERTS), jnp.int32)],
        grid=(b, nt),
        in_specs=[tok(FOURIER_WIDTH), tok(DIFF_V_WIDTH), tok(d), pl.BlockSpec((1, N_MOD, d), lambda bi, i: (bi, 0, 0)),
                  full(g_post), full(g_pre), full(w_out_b), full(w_router), full(rb_col), full(wsg), full(wsu), full(wsd)],
        out_specs=[tok(d), tok(d), tok(d), plan(TOP_K, tm), plan(TOP_K, tm), plan(GROUP_SIZE, N_EXPERTS)],
        compiler_params=_cparams(("parallel", "parallel")),
        name="mix",
    )(four, att, x, mod, g_post, g_pre, w_out_b, w_router, rb_col, wsg, wsu, wsd)


def _moe_plan(cp):
    n_tiles = cp.shape[0]
    ends_loc = jnp.cumsum(cp, axis=1)
    seg_loc = ends_loc - cp
    reg = (jnp.sum(cp, axis=0) + MOE_BLOCK - 1) // MOE_BLOCK * MOE_BLOCK
    reg_end = jnp.cumsum(reg)
    seg_glob = (reg_end - reg)[None, :] + jnp.cumsum(cp, axis=0) - cp
    g_row = jnp.arange(LOC_ROWS // SEG, dtype=jnp.int32) * SEG
    e_of_g = jnp.sum((ends_loc[:, None, :] <= g_row[None, :, None]).astype(jnp.int32), axis=-1)
    delta = jnp.take_along_axis(seg_glob - seg_loc, jnp.minimum(e_of_g, N_EXPERTS - 1), axis=1)
    n_glob = _moe_rows(n_tiles * MOE_TILE)
    gdst = jnp.clip((delta + g_row[None, :]) // SEG, 0, n_glob // SEG - 1)
    ngran = ends_loc[:, -1] // SEG
    n_blocks = reg_end[-1] // MOE_BLOCK
    blk = jnp.minimum(jnp.arange(n_glob // MOE_BLOCK, dtype=jnp.int32), n_blocks - 1) * MOE_BLOCK
    blk_expert = jnp.minimum(jnp.sum((reg_end[None, :] <= blk[:, None]).astype(jnp.int32), axis=1), N_EXPERTS - 1)
    seg_tot = jnp.sum(cp, axis=0)
    k = jnp.arange(TAIL_SLOTS, dtype=jnp.int32)
    tail_gran = jnp.clip(((reg_end - reg + seg_tot) // SEG)[:, None] + k[None, :], 0, n_glob // SEG - 1)
    tail_ok = k[None, :] < ((reg - seg_tot) // SEG)[:, None]
    i32 = lambda a: a.astype(jnp.int32)
    return {"gdst": i32(gdst.reshape(-1)), "ngran": i32(ngran), "blk_expert": i32(blk_expert),
            "n_blocks": i32(n_blocks.reshape(1)), "tail_gran": i32(tail_gran.reshape(-1)),
            "tail_ok": i32(tail_ok.reshape(-1)), "n_tail": i32(jnp.sum(tail_ok).reshape(1))}


def _moe_rows(n_tok):
    rows = n_tok * TOP_K + (n_tok // MOE_TILE) * N_EXPERTS * (SEG - 1) + N_EXPERTS * (MOE_BLOCK - 1)
    return -(-rows // MOE_BLOCK) * MOE_BLOCK


def _dispatch_kernel(gdst_ref, ngran_ref, tail_gran_ref, tail_ok_ref, n_tail_ref, pos_ref, h_ref, xs_ref,
                     xloc_ref, zero_ref, sem_ref):
    i = pl.program_id(0)
    last = pl.num_programs(0) - 1
    buf = lax.rem(i, 2)
    gl = LOC_ROWS // SEG

    def granule(b, g_loc, g_glob):
        return pltpu.make_async_copy(xloc_ref.at[b, pl.ds(g_loc * SEG, SEG), :],
                                     xs_ref.at[pl.ds(g_glob * SEG, SEG), :], sem_ref.at[b])

    def tail_granule(g_glob):
        return pltpu.make_async_copy(zero_ref, xs_ref.at[pl.ds(g_glob * SEG, SEG), :], sem_ref.at[2])

    def wait_tile(tile, b):
        lax.fori_loop(0, ngran_ref[tile], lambda g, c: (granule(b, 0, 0).wait(), c)[1], 0)

    @pl.when(i == 0)
    def _():
        zero_ref[...] = jnp.zeros(zero_ref.shape, BF16)

        def body(s, c):
            @pl.when(tail_ok_ref[s] == 1)
            def _():
                tail_granule(tail_gran_ref[s]).start()

            return c

        lax.fori_loop(0, N_EXPERTS * TAIL_SLOTS, body, 0)

    @pl.when(i >= 2)
    def _():
        wait_tile(i - 2, buf)

    h = h_ref[...]
    pos = pos_ref[0]
    used = ngran_ref[i] * SEG
    def gather_block(rb):
        r = lax.broadcasted_iota(jnp.int32, (LOC_BLOCK, MOE_TILE), 0) + rb * LOC_BLOCK
        hit = r == pos[0:1, :]
        for j in range(1, TOP_K):
            hit = jnp.logical_or(hit, r == pos[j:j + 1, :])
        onehot = jnp.where(hit, 1.0, 0.0).astype(BF16)
        xloc_ref[buf, rb * LOC_BLOCK:(rb + 1) * LOC_BLOCK, :] = jnp.dot(
            onehot, h, preferred_element_type=F32).astype(BF16)

    for rb in range(LOC_ROWS // LOC_BLOCK):
        if rb < ALWAYS_USED_BLOCKS:
            gather_block(rb)
        else:
            pl.when(rb * LOC_BLOCK < used)(functools.partial(gather_block, rb))

    def issue(g, c):
        granule(buf, g, gdst_ref[i * gl + g]).start()
        return c

    lax.fori_loop(0, ngran_ref[i], issue, 0)

    @pl.when(i == last)
    def _():
        wait_tile(i, buf)

        @pl.when(i >= 1)
        def _():
            wait_tile(i - 1, 1 - buf)

        lax.fori_loop(0, n_tail_ref[0], lambda s, c: (tail_granule(0).wait(), c)[1], 0)


def _dispatch(h2, pos, plan):
    n_tok, d = h2.shape
    n_tiles = n_tok // MOE_TILE
    grid_spec = pltpu.PrefetchScalarGridSpec(
        num_scalar_prefetch=5,
        grid=(n_tiles,),
        in_specs=[pl.BlockSpec((1, TOP_K, MOE_TILE), lambda i, *_: (i, 0, 0)),
                  pl.BlockSpec((MOE_TILE, d), lambda i, *_: (i, 0))],
        out_specs=pl.BlockSpec(memory_space=pl.ANY),
        scratch_shapes=[pltpu.VMEM((2, LOC_ROWS, d), BF16), pltpu.VMEM((SEG, d), BF16),
                        pltpu.SemaphoreType.DMA((3,))],
    )
    return pl.pallas_call(
        _dispatch_kernel,
        out_shape=jax.ShapeDtypeStruct((_moe_rows(n_tok), d), BF16),
        grid_spec=grid_spec,
        compiler_params=_cparams(("arbitrary",)),
        name="moe_dispatch",
    )(plan["gdst"], plan["ngran"], plan["tail_gran"], plan["tail_ok"], plan["n_tail"], pos, h2)


def _expert_kernel(be_ref, nb_ref, x_ref, wg_ref, wu_ref, wd_ref, y_ref):
    @pl.when(pl.program_id(0) < nb_ref[0])
    def _():
        x = x_ref[...]
        a = jnp.dot(x, wg_ref[0], preferred_element_type=F32)
        u = jnp.dot(x, wu_ref[0], preferred_element_type=F32)
        hid = (a * jax.nn.sigmoid(a)) * u
        y_ref[...] = jnp.dot(hid.astype(BF16), wd_ref[0], preferred_element_type=F32).astype(BF16)


def _experts(xs, blk_expert, n_blocks, wg, wu, wd):
    rows, d = xs.shape
    de = wg.shape[-1]
    live = lambda i, be, nb: jnp.minimum(i, nb[0] - 1)
    grid_spec = pltpu.PrefetchScalarGridSpec(
        num_scalar_prefetch=2,
        grid=(rows // MOE_BLOCK,),
        in_specs=[pl.BlockSpec((MOE_BLOCK, d), lambda i, be, nb: (live(i, be, nb), 0)),
                  pl.BlockSpec((1, d, de), lambda i, be, nb: (be[i], 0, 0)),
                  pl.BlockSpec((1, d, de), lambda i, be, nb: (be[i], 0, 0)),
                  pl.BlockSpec((1, de, d), lambda i, be, nb: (be[i], 0, 0))],
        out_specs=pl.BlockSpec((MOE_BLOCK, d), lambda i, be, nb: (live(i, be, nb), 0)),
    )
    return pl.pallas_call(
        _expert_kernel,
        out_shape=jax.ShapeDtypeStruct((rows, d), BF16),
        grid_spec=grid_spec,
        compiler_params=_cparams(("arbitrary",)),
        name="moe_experts",
    )(blk_expert, n_blocks, xs, wg, wu, wd)


def _combine_kernel(gdst_ref, ngran_ref, ys_ref, pos_ref, wts_ref, sh_ref, x1_ref, mod_ref, g_ref, o_ref,
                    yloc_ref, acc_ref, sem_ref):
    i = pl.program_id(0)
    last = pl.num_programs(0) - 1
    buf = lax.rem(i, 2)
    gl = LOC_ROWS // SEG

    def granule(b, g_loc, g_glob):
        return pltpu.make_async_copy(ys_ref.at[pl.ds(g_glob * SEG, SEG), :],
                                     yloc_ref.at[b, pl.ds(g_loc * SEG, SEG), :], sem_ref.at[b])

    def fetch(tile, b):
        def body(g, c):
            granule(b, g, gdst_ref[tile * gl + g]).start()
            return c

        lax.fori_loop(0, ngran_ref[tile], body, 0)

    @pl.when(i == 0)
    def _():
        yloc_ref[...] = jnp.zeros(yloc_ref.shape, BF16)
        fetch(0, 0)

    @pl.when(i < last)
    def _():
        fetch(i + 1, 1 - buf)

    lax.fori_loop(0, ngran_ref[i], lambda g, c: (granule(buf, 0, 0).wait(), c)[1], 0)

    pos = pos_ref[...]
    wts = wts_ref[...]
    used = ngran_ref[i] * SEG
    lane = lax.broadcasted_iota(jnp.int32, (MOE_TILE, LANES), 1)
    pos_b = [jnp.broadcast_to(pos[:, j:j + 1], (MOE_TILE, LANES)) for j in range(TOP_K)]
    wts_b = [jnp.broadcast_to(wts[:, j:j + 1], (MOE_TILE, LANES)) for j in range(TOP_K)]

    def weighted_block(cb):
        pieces = []
        for c in range(LOC_BLOCK // LANES):
            target = lane + (cb * LOC_BLOCK + c * LANES)
            w = jnp.zeros((MOE_TILE, LANES), F32)
            for j in range(TOP_K):
                w = jnp.where(pos_b[j] == target, wts_b[j], w)
            pieces.append(w.astype(BF16))
        return jnp.dot(jnp.concatenate(pieces, axis=1), yloc_ref[buf, cb * LOC_BLOCK:(cb + 1) * LOC_BLOCK, :],
                       preferred_element_type=F32)

    def add_block(cb):
        acc_ref[...] += weighted_block(cb)

    acc = sh_ref[...]
    for cb in range(ALWAYS_USED_BLOCKS):
        acc = acc + weighted_block(cb)
    acc_ref[...] = acc
    for cb in range(ALWAYS_USED_BLOCKS, LOC_ROWS // LOC_BLOCK):
        pl.when(cb * LOC_BLOCK < used)(functools.partial(add_block, cb))

    o_ref[...] = x1_ref[...] + mod_ref[0, 5:6, :] * (_rms(acc_ref[...]) * g_ref[...])


def _combine(ys, pos_t, wts_t, shared, x1, mod, g_post, gdst, ngran, tiles_per_batch):
    n_tok, d = x1.shape
    n_tiles = n_tok // MOE_TILE
    tok = lambda w: pl.BlockSpec((MOE_TILE, w), lambda i, *_: (i, 0))
    grid_spec = pltpu.PrefetchScalarGridSpec(
        num_scalar_prefetch=2,
        grid=(n_tiles,),
        in_specs=[pl.BlockSpec(memory_space=pl.ANY), tok(TOP_K), tok(TOP_K), tok(d), tok(d),
                  pl.BlockSpec((1, N_MOD, d), lambda i, *_: (i // tiles_per_batch, 0, 0)),
                  pl.BlockSpec(g_post.shape, lambda i, *_: (0, 0))],
        out_specs=tok(d),
        scratch_shapes=[pltpu.VMEM((2, LOC_ROWS, d), BF16), pltpu.VMEM((MOE_TILE, d), F32),
                        pltpu.SemaphoreType.DMA((2,))],
    )
    return pl.pallas_call(
        _combine_kernel,
        out_shape=jax.ShapeDtypeStruct((n_tok, d), F32),
        grid_spec=grid_spec,
        compiler_params=_cparams(("arbitrary",)),
        name="moe_combine",
    )(gdst, ngran, ys, pos_t, wts_t, shared, x1, mod, g_post)


def kernel(x, c, ctx, c_ctx, w_mod, b_mod, g_attn_pre, g_attn_post, w_in, w_fourier, lambda_q1, lambda_k1,
           lambda_q2, lambda_k2, g_subln, w_out, g_ffn_pre, g_ffn_post, w_router, router_bias, w_gate, w_up,
           w_down, ws_gate, ws_up, ws_down):
    b, n, d = x.shape
    assert w_mod.shape[0] == 1, "single-layer kernel"
    assert n % (DFT_N1 * DFT_KGROUP) == 0 and b + 1 <= 8
    row = lambda a: a[0].reshape(1, -1)

    cond8 = jnp.zeros((8, d), F32).at[:b].set(c).at[b].set(c_ctx)
    mod = _ada(cond8, w_mod[0], b_mod[0]).reshape(8, N_MOD, d)

    w_in_b = w_in[0].astype(BF16)
    cc, sc = _channel_dft(n)
    wf = jnp.zeros((FOURIER_GROUPS, FOURIER_GROUP_DIM, FOURIER_GROUPS, FOURIER_GROUP_DIM), F32)
    for g in range(FOURIER_GROUPS):
        wf = wf.at[g, :, g, :].set(w_fourier[0, g])
    wf = wf.reshape(FOURIER_WIDTH, FOURIER_WIDTH).astype(BF16)

    tiles = _tiles(n, ctx.shape[1])
    u, v2, q, k_all, v_all = _prep(x, ctx, mod, b, row(g_attn_pre), w_in_b, _rope_tables(n), cc, sc, wf)

    m_u, m_v = _dft_stage1(n // DFT_N1)
    a2, b2 = _dft_stage2(n)
    four = _fourier(u, v2, m_u, m_v, a2, b2)

    lam_vecs = jnp.concatenate([lambda_q1, lambda_k1, lambda_q2, lambda_k2], axis=0).astype(F32)
    att = _attention(q, k_all, v_all, lam_vecs, row(g_subln), tiles["attn_q"], tiles["attn_kv"])

    x1, h2, shared, pos, wts, cp = _mix(four, att, x, mod, row(g_attn_post), row(g_ffn_pre), w_out[0].astype(BF16),
                                        w_router[0], router_bias[0].reshape(-1, 1), ws_gate[0].astype(BF16),
                                        ws_up[0].astype(BF16), ws_down[0].astype(BF16))
    plan = _moe_plan(cp[:, 0, :])
    xs = _dispatch(h2.reshape(b * n, d), pos, plan)
    ys = _experts(xs, plan["blk_expert"], plan["n_blocks"], w_gate[0].astype(BF16), w_up[0].astype(BF16),
                  w_down[0].astype(BF16))
    token_major = lambda a: a.transpose(0, 2, 1).reshape(b * n, TOP_K)
    out = _combine(ys, token_major(pos), token_major(wts), shared.reshape(b * n, d), x1.reshape(b * n, d), mod,
                   row(g_ffn_post), plan["gdst"], plan["ngran"], n // MOE_TILE)
    return out.reshape(b, n, d)
```

```python
import functools
import math

import numpy as np
import jax
import jax.numpy as jnp
from jax import lax
from jax.experimental import pallas as pl
from jax.experimental.pallas import tpu as pltpu

GRID_W = 64
N_MOD = 6
NORM_EPS = 1e-6
FOURIER_GROUPS = 4
FOURIER_GROUP_DIM = 64
FOURIER_WIDTH = FOURIER_GROUPS * FOURIER_GROUP_DIM
DIFF_HEADS = 6
DIFF_HEAD_DIM = 64
DIFF_VALUE_DIM = 2 * DIFF_HEAD_DIM
DIFF_QK_WIDTH = DIFF_HEADS * 2 * DIFF_HEAD_DIM
DIFF_V_WIDTH = DIFF_HEADS * DIFF_VALUE_DIM
ROPE_BASE = 10000.0
ROPE_AXIS_DIM = DIFF_HEAD_DIM // 2
N_EXPERTS = 64
TOP_K = 8
N_EXPERT_GROUPS = 8
TOP_K_GROUPS = 4
GROUP_SIZE = N_EXPERTS // N_EXPERT_GROUPS
ROUTED_SCALE = 2.5
LAM_INIT = 0.8 - 0.6 * math.exp(-0.3 * 0)
Q_SCALE = DIFF_HEAD_DIM ** -0.5

DFT_N1 = 64
DFT_KGROUP = 8
LANES = 128
VMEM_LIMIT = 56 * 1024 * 1024

BF16 = jnp.bfloat16
F32 = jnp.float32


MOE_TILE = 256
MIX_TILES_PER_STEP = 2
SEG = 16
MOE_BLOCK = 1024
TAIL_SLOTS = MOE_BLOCK // SEG - 1
LOC_BLOCK = 512
LOC_ROWS = -(-(MOE_TILE * TOP_K + N_EXPERTS * (SEG - 1)) // LOC_BLOCK) * LOC_BLOCK
ALWAYS_USED_BLOCKS = MOE_TILE * TOP_K // LOC_BLOCK

MAX_UNROLLED_KV_CHUNKS = 12
ATTN_KV_TARGET = 1408


def _tiles(n, n_ctx):
    n_keys = n + n_ctx
    kv = max(t for t in range(LANES, n_keys + 1, LANES) if n_keys % t == 0 and t <= ATTN_KV_TARGET)
    return {"attn_q": min(n, 512), "attn_kv": kv}


def _cparams(sem):
    return pltpu.CompilerParams(dimension_semantics=sem, vmem_limit_bytes=VMEM_LIMIT)


def _bdot(a, b):
    return jnp.dot(a.astype(BF16), b.astype(BF16), preferred_element_type=F32)


def _rms(x):
    return x * lax.rsqrt(jnp.mean(x * x, axis=-1, keepdims=True) + NORM_EPS)


_TWO_PI_HI = float(np.float32(2.0 * np.pi))
_TWO_PI_LO = float(np.float32(2.0 * np.pi - _TWO_PI_HI))


def _cos_sin(phase, period):
    t = phase.astype(F32) / period
    ang = t * _TWO_PI_HI + t * _TWO_PI_LO
    return jnp.cos(ang), jnp.sin(ang)


def _rope_tables(n_tokens):
    rows = n_tokens // GRID_W
    row = jnp.broadcast_to(jnp.arange(rows)[:, None], (rows, GRID_W)).reshape(-1)
    col = jnp.broadcast_to(jnp.arange(GRID_W)[None, :], (rows, GRID_W)).reshape(-1)
    inv_freq = ROPE_BASE ** (-jnp.arange(0, ROPE_AXIS_DIM, 2, dtype=F32) / ROPE_AXIS_DIM)
    pos = jnp.stack([row, col], axis=-1).astype(F32)
    ang = pos[:, :, None] * inv_freq
    cos, sin = jnp.cos(ang), jnp.sin(ang)
    z = jnp.zeros_like(sin[:, 0])
    reps = LANES // DIFF_HEAD_DIM
    cos_l = jnp.tile(jnp.concatenate([cos[:, 0], cos[:, 0], cos[:, 1], cos[:, 1]], axis=-1), (1, reps))
    s_next = jnp.tile(jnp.concatenate([-sin[:, 0], z, -sin[:, 1], z], axis=-1), (1, reps))
    s_prev = jnp.tile(jnp.concatenate([z, sin[:, 0], z, sin[:, 1]], axis=-1), (1, reps))
    return cos_l, s_next, s_prev


def _channel_dft(n_tokens):
    c = jnp.arange(FOURIER_GROUP_DIM, dtype=jnp.int32)
    cs, sn = _cos_sin((c[:, None] * c[None, :]) % FOURIER_GROUP_DIM, FOURIER_GROUP_DIM)
    norm = lax.rsqrt(jnp.float32(n_tokens * FOURIER_GROUP_DIM))
    eye = jnp.eye(FOURIER_GROUPS, dtype=F32)
    return jnp.kron(eye, cs * norm).astype(BF16), jnp.kron(eye, sn * norm).astype(BF16)


def _dft_stage1(n2):
    k = jnp.arange(n2, dtype=jnp.int32)
    c, s = _cos_sin((k[:, None] * k[None, :]) % n2, n2)
    m_u = jnp.concatenate([c, -s], axis=0)
    m_v = jnp.concatenate([-s, -c], axis=0)
    return m_u.astype(BF16), m_v.astype(BF16)


def _dft_stage2(n_tokens):
    n2 = n_tokens // DFT_N1
    groups = n2 // DFT_KGROUP
    k1 = jnp.arange(DFT_N1, dtype=jnp.int32)[None, :, None, None]
    n1 = jnp.arange(DFT_N1, dtype=jnp.int32)[None, None, None, :]
    k2 = (jnp.arange(groups, dtype=jnp.int32)[:, None, None, None] * DFT_KGROUP
          + jnp.arange(DFT_KGROUP, dtype=jnp.int32)[None, None, :, None])
    c, s = _cos_sin((k1 * n1 * n2 + k2 * n1) % n_tokens, n_tokens)
    eye = jnp.eye(DFT_KGROUP, dtype=F32)[None, None, :, :, None]
    r = DFT_N1 * DFT_KGROUP
    a = (c[:, :, :, None, :] * eye).reshape(groups, r, r)
    b = (s[:, :, :, None, :] * eye).reshape(groups, r, r)
    return a.astype(BF16), b.astype(BF16)


def _ada_kernel(c_ref, w_ref, b_ref, o_ref):
    c = c_ref[...]
    s = c * jax.nn.sigmoid(c)
    o_ref[...] = jnp.dot(s, w_ref[...], preferred_element_type=F32) + b_ref[...]


def _ada(cond8, w_mod, b_mod):
    d = cond8.shape[1]
    return pl.pallas_call(
        _ada_kernel,
        out_shape=jax.ShapeDtypeStruct((8, N_MOD * d), F32),
        grid=(N_MOD,),
        in_specs=[pl.BlockSpec((8, d), lambda j: (0, 0)),
                  pl.BlockSpec((d, d), lambda j: (0, j)),
                  pl.BlockSpec((1, d), lambda j: (0, j))],
        out_specs=pl.BlockSpec((8, d), lambda j: (0, j)),
        compiler_params=_cparams(("arbitrary",)),
        name="ada",
    )(cond8, w_mod, b_mod.reshape(1, -1))


def _rope_chunk(ch, cos, s_next, s_prev):
    nxt = pltpu.roll(ch, LANES - ROPE_AXIS_DIM // 2, 1)
    prv = pltpu.roll(ch, ROPE_AXIS_DIM // 2, 1)
    return ch * cos + nxt * s_next + prv * s_prev


def _modulated(x, g_ref, mod_ref):
    h = _rms(x) * g_ref[...]
    return (h * (1.0 + mod_ref[0, 1:2, :]) + mod_ref[0, 0:1, :]).astype(BF16)


def _prep_kernel(x_ref, ctx_ref, mod_ref, modc_ref, g_ref, w_ref, cos_ref, sn_ref, sp_ref, cc_ref, sc_ref, wf_ref,
                 u_ref, v2_ref, q_ref, k_ref, v_ref):
    o1 = FOURIER_WIDTH
    o2 = o1 + DIFF_QK_WIDTH
    o3 = o2 + DIFF_QK_WIDTH
    step = pl.program_id(1)

    @pl.when(step == 0)
    def _():
        hb = _modulated(ctx_ref[0], g_ref, modc_ref)
        k_ref[0] = jnp.dot(hb, w_ref[:, o2:o3], preferred_element_type=F32).astype(BF16)
        v_ref[0] = jnp.dot(hb, w_ref[:, o3:], preferred_element_type=F32).astype(BF16)

    @pl.when(step > 0)
    def _():
        hb = _modulated(x_ref[0], g_ref, mod_ref)
        f = jnp.dot(hb, w_ref[:, 0:o1], preferred_element_type=F32).astype(BF16)
        fc = jnp.dot(f, cc_ref[...], preferred_element_type=F32).astype(BF16)
        fs = jnp.dot(f, sc_ref[...], preferred_element_type=F32).astype(BF16)
        u_ref[0] = jnp.dot(fc, wf_ref[...], preferred_element_type=F32).astype(BF16)
        v2_ref[0] = jnp.dot(fs, wf_ref[...], preferred_element_type=F32).astype(BF16)
        cos, sn, sp = cos_ref[...], sn_ref[...], sp_ref[...]
        for j in range(DIFF_QK_WIDTH // LANES):
            lo = j * LANES
            qc = jnp.dot(hb, w_ref[:, o1 + lo:o1 + lo + LANES], preferred_element_type=F32)
            q_ref[0, :, lo:lo + LANES] = (_rope_chunk(qc, cos, sn, sp) * Q_SCALE).astype(BF16)
            kc = jnp.dot(hb, w_ref[:, o2 + lo:o2 + lo + LANES], preferred_element_type=F32)
            k_ref[0, :, lo:lo + LANES] = _rope_chunk(kc, cos, sn, sp).astype(BF16)
        v_ref[0] = jnp.dot(hb, w_ref[:, o3:], preferred_element_type=F32).astype(BF16)


def _prep(x, ctx, mod, ctx_row, g, w_in_b, rope, cc, sc, wf):
    b, n, d = x.shape
    tm = ctx.shape[1]
    assert n % tm == 0
    cos, sn, sp = rope
    lat = lambda i: jnp.maximum(i - 1, 0)
    tok = lambda w: pl.BlockSpec((1, tm, w), lambda bi, i: (bi, lat(i), 0))
    allk = lambda w: pl.BlockSpec((1, tm, w), lambda bi, i: (bi, i, 0))
    full = lambda a: pl.BlockSpec(a.shape, lambda bi, i: (0,) * a.ndim)
    tab = pl.BlockSpec((tm, LANES), lambda bi, i: (lat(i), 0))
    outs = [jax.ShapeDtypeStruct((b, n, w), BF16) for w in (FOURIER_WIDTH, FOURIER_WIDTH, DIFF_QK_WIDTH)]
    outs += [jax.ShapeDtypeStruct((b, tm + n, w), BF16) for w in (DIFF_QK_WIDTH, DIFF_V_WIDTH)]
    return pl.pallas_call(
        _prep_kernel,
        out_shape=outs,
        grid=(b, 1 + n // tm),
        in_specs=[tok(d), pl.BlockSpec((1, tm, d), lambda bi, i: (bi, 0, 0)),
                  pl.BlockSpec((1, N_MOD, d), lambda bi, i: (bi, 0, 0)),
                  pl.BlockSpec((1, N_MOD, d), lambda bi, i: (ctx_row, 0, 0)),
                  full(g), full(w_in_b), tab, tab, tab, full(cc), full(sc), full(wf)],
        out_specs=[tok(FOURIER_WIDTH), tok(FOURIER_WIDTH), tok(DIFF_QK_WIDTH), allk(DIFF_QK_WIDTH), allk(DIFF_V_WIDTH)],
        compiler_params=_cparams(("parallel", "arbitrary")),
        name="prep",
    )(x, ctx, mod, mod, g, w_in_b, cos, sn, sp, cc, sc, wf)


def _dft1_kernel(u_ref, v_ref, mu_ref, mv_ref, yr_ref, yi_ref):
    n2 = u_ref.shape[1]
    y = (jnp.dot(mu_ref[...], u_ref[0], preferred_element_type=F32)
         + jnp.dot(mv_ref[...], v_ref[0], preferred_element_type=F32))
    yr_ref[0] = y[:n2].astype(BF16)
    yi_ref[0] = y[n2:].astype(BF16)


def _dft2_kernel(yr_ref, yi_ref, a_ref, b_ref, o_ref):
    y = (jnp.dot(a_ref[0], yr_ref[0], preferred_element_type=F32)
         + jnp.dot(b_ref[0], yi_ref[0], preferred_element_type=F32))
    o_ref[0] = y.reshape(DFT_N1, DFT_KGROUP, FOURIER_WIDTH).astype(BF16)


def _fourier(u, v, m_u, m_v, a2, b2):
    b, n, w = u.shape
    n2 = n // DFT_N1
    cols = DFT_N1 * w
    tn = min(cols, 2048)
    u2 = u.reshape(b, n2, cols)
    v2 = v.reshape(b, n2, cols)
    blk = pl.BlockSpec((1, n2, tn), lambda bi, j: (bi, 0, j))
    mat = pl.BlockSpec((2 * n2, n2), lambda bi, j: (0, 0))
    yr, yi = pl.pallas_call(
        _dft1_kernel,
        out_shape=[jax.ShapeDtypeStruct((b, n2, cols), BF16)] * 2,
        grid=(b, cols // tn),
        in_specs=[blk, blk, mat, mat],
        out_specs=[blk, blk],
        compiler_params=_cparams(("parallel", "parallel")),
        name="dft_stage1",
    )(u2, v2, m_u, m_v)
    yr = yr.reshape(b, n, w)
    yi = yi.reshape(b, n, w)
    rows = DFT_N1 * DFT_KGROUP
    groups = n2 // DFT_KGROUP
    yblk = pl.BlockSpec((1, rows, w), lambda bi, gi: (bi, gi, 0))
    mblk = pl.BlockSpec((1, rows, rows), lambda bi, gi: (gi, 0, 0))
    out = pl.pallas_call(
        _dft2_kernel,
        out_shape=jax.ShapeDtypeStruct((b, DFT_N1, n2, w), BF16),
        grid=(b, groups),
        in_specs=[yblk, yblk, mblk, mblk],
        out_specs=pl.BlockSpec((1, DFT_N1, DFT_KGROUP, w), lambda bi, gi: (bi, 0, gi, 0)),
        compiler_params=_cparams(("parallel", "parallel")),
        name="dft_stage2",
    )(yr, yi, a2, b2)
    return out.reshape(b, n, w)


def _attn_kernel(q_ref, k_ref, v_ref, lam_ref, g_ref, o_ref,
                 qs_ref, va_ref, m_ref, acc_ref, *, tk):
    tq = q_ref.shape[1]
    n_keys = k_ref.shape[1]

    @pl.when(pl.program_id(2) == 0)
    def _():
        va_ref[:, 0:LANES] = v_ref[0]
        va_ref[:, LANES:] = jnp.ones((n_keys, LANES), BF16)

    q = q_ref[0]
    lane = lax.broadcasted_iota(jnp.int32, (tq, LANES), 1)
    zero = jnp.zeros_like(q)
    qs_ref[0:tq, :] = jnp.where(lane < DIFF_HEAD_DIM, q, zero)
    qs_ref[tq:, :] = jnp.where(lane >= DIFF_HEAD_DIM, q, zero)
    m_ref[...] = jnp.full(m_ref.shape, -jnp.inf, F32)
    acc_ref[...] = jnp.zeros(acc_ref.shape, F32)

    def chunk(k, va):
        s = lax.dot_general(qs_ref[...], k, (((1,), (1,)), ((), ())), preferred_element_type=F32)
        cols = [s[:, c * LANES:(c + 1) * LANES] for c in range(k.shape[0] // LANES)]
        mx = cols[0]
        for col in cols[1:]:
            mx = jnp.maximum(mx, col)
        m_prev = m_ref[...]
        m_new = jnp.maximum(m_prev, jnp.max(mx, axis=1, keepdims=True))
        alpha = jnp.exp(m_prev - m_new)
        p = jnp.concatenate([jnp.exp(col - m_new).astype(BF16) for col in cols], axis=1)
        pv = jnp.dot(p, va, preferred_element_type=F32)
        acc_ref[...] = jnp.concatenate([alpha, alpha], axis=1) * acc_ref[...] + pv
        m_ref[...] = m_new

    n_chunks = n_keys // tk
    if n_chunks <= MAX_UNROLLED_KV_CHUNKS:
        for j in range(n_chunks):
            chunk(k_ref[0, j * tk:(j + 1) * tk, :], va_ref[j * tk:(j + 1) * tk, :])
    else:
        def body(j, carry):
            start = pl.multiple_of(j * tk, tk)
            chunk(k_ref[0, pl.ds(start, tk), :], va_ref[pl.ds(start, tk), :])
            return carry

        lax.fori_loop(0, n_chunks, body, 0)

    lam = (jnp.exp(jnp.sum(lam_ref[0:1, :] * lam_ref[1:2, :], axis=-1, keepdims=True))
           - jnp.exp(jnp.sum(lam_ref[2:3, :] * lam_ref[3:4, :], axis=-1, keepdims=True)) + LAM_INIT)
    o = (acc_ref[0:tq, 0:LANES] / acc_ref[0:tq, LANES:]
         - lam * (acc_ref[tq:, 0:LANES] / acc_ref[tq:, LANES:]))
    o_ref[0] = (_rms(o) * g_ref[...] * (1.0 - LAM_INIT)).astype(BF16)


def _attention(q, k_all, v_all, lam_vecs, g_subln, tq, tk):
    b, n, _ = q.shape
    n_keys = k_all.shape[1]
    assert n_keys % tk == 0
    qspec = pl.BlockSpec((1, tq, LANES), lambda bi, h, i: (bi, i, h))
    kspec = pl.BlockSpec((1, n_keys, LANES), lambda bi, h, i: (bi, 0, h))
    return pl.pallas_call(
        functools.partial(_attn_kernel, tk=tk),
        out_shape=jax.ShapeDtypeStruct((b, n, DIFF_V_WIDTH), BF16),
        grid=(b, DIFF_HEADS, n // tq),
        in_specs=[qspec, kspec, kspec,
                  pl.BlockSpec(lam_vecs.shape, lambda bi, h, i: (0, 0)),
                  pl.BlockSpec(g_subln.shape, lambda bi, h, i: (0, 0))],
        out_specs=qspec,
        scratch_shapes=[pltpu.VMEM((2 * tq, LANES), BF16), pltpu.VMEM((n_keys, 2 * LANES), BF16),
                        pltpu.VMEM((2 * tq, LANES), F32), pltpu.VMEM((2 * tq, 2 * LANES), F32)],
        compiler_params=_cparams(("parallel", "parallel", "arbitrary")),
        name="diff_attention",
    )(q, k_all, v_all, lam_vecs, g_subln)


def _route(logits_t, bias_col):
    tm = logits_t.shape[1]
    scores = jax.nn.sigmoid(logits_t)
    sel = scores + bias_col
    ninf = jnp.float32(-jnp.inf)
    i8 = lax.broadcasted_iota(jnp.int32, (GROUP_SIZE, tm), 0)
    slabs = [sel[GROUP_SIZE * g:GROUP_SIZE * (g + 1)] for g in range(N_EXPERT_GROUPS)]
    rows = []
    for slab in slabs:
        m1 = jnp.max(slab, axis=0, keepdims=True)
        i1 = jnp.min(jnp.where(slab == m1, i8, GROUP_SIZE), axis=0, keepdims=True)
        m2 = jnp.max(jnp.where(i8 == i1, ninf, slab), axis=0, keepdims=True)
        rows.append(m1 + m2)
    gs = jnp.concatenate(rows, axis=0)
    gmask = jnp.zeros(gs.shape, jnp.bool_)
    for _ in range(TOP_K_GROUPS):
        m = jnp.max(gs, axis=0, keepdims=True)
        i = jnp.min(jnp.where(gs == m, i8, N_EXPERT_GROUPS), axis=0, keepdims=True)
        hit = i8 == i
        gmask = jnp.logical_or(gmask, hit)
        gs = jnp.where(hit, ninf, gs)
    gmask_f = gmask.astype(F32)
    cand = [jnp.where(gmask_f[g:g + 1] > 0.5, slabs[g], ninf) for g in range(N_EXPERT_GROUPS)]
    chosen = [jnp.zeros((GROUP_SIZE, tm), jnp.bool_) for _ in range(N_EXPERT_GROUPS)]
    picks = []
    for _ in range(TOP_K):
        mm = cand[0]
        for g in range(1, N_EXPERT_GROUPS):
            mm = jnp.maximum(mm, cand[g])
        m = jnp.max(mm, axis=0, keepdims=True)
        ii = jnp.where(cand[0] == m, i8, N_EXPERTS)
        for g in range(1, N_EXPERT_GROUPS):
            ii = jnp.minimum(ii, jnp.where(cand[g] == m, i8 + GROUP_SIZE * g, N_EXPERTS))
        idx = jnp.min(ii, axis=0, keepdims=True)
        picks.append(idx)
        for g in range(N_EXPERT_GROUPS):
            hit = (i8 + GROUP_SIZE * g) == idx
            chosen[g] = jnp.logical_or(chosen[g], hit)
            cand[g] = jnp.where(hit, ninf, cand[g])
    score_slabs = [scores[GROUP_SIZE * g:GROUP_SIZE * (g + 1)] for g in range(N_EXPERT_GROUPS)]
    tot = jnp.where(chosen[0], score_slabs[0], 0.0)
    for g in range(1, N_EXPERT_GROUPS):
        tot = tot + jnp.where(chosen[g], score_slabs[g], 0.0)
    denom = jnp.sum(tot, axis=0, keepdims=True) + 1e-20

    chosen_f = jnp.concatenate([c.astype(F32) for c in chosen], axis=0)
    chosen_b = chosen_f.astype(BF16)
    earlier = (lax.broadcasted_iota(jnp.int32, (tm, tm), 0) < lax.broadcasted_iota(jnp.int32, (tm, tm), 1))
    rank = jnp.dot(chosen_b, earlier.astype(BF16), preferred_element_type=F32)
    pad = lambda cnt: jnp.floor((cnt + (SEG - 1)) * (1.0 / SEG)) * SEG
    cp_col = pad(jnp.sum(chosen_f, axis=1, keepdims=True))
    lower = (lax.broadcasted_iota(jnp.int32, (N_EXPERTS, N_EXPERTS), 1)
             < lax.broadcasted_iota(jnp.int32, (N_EXPERTS, N_EXPERTS), 0))
    seg_start = jnp.dot(lower.astype(BF16), jnp.broadcast_to(cp_col, (N_EXPERTS, LANES)).astype(BF16),
                        preferred_element_type=F32)[:, 0:1]
    base = seg_start + rank
    base_slabs = [base[GROUP_SIZE * g:GROUP_SIZE * (g + 1)] for g in range(N_EXPERT_GROUPS)]

    def at_pick(slabs_, idx):
        acc = jnp.zeros((1, tm), F32)
        for g in range(N_EXPERT_GROUPS):
            acc = acc + jnp.sum(jnp.where((i8 + GROUP_SIZE * g) == idx, slabs_[g], 0.0), axis=0, keepdims=True)
        return acc

    pos = jnp.concatenate([at_pick(base_slabs, idx) for idx in picks], axis=0).astype(jnp.int32)
    wts = jnp.concatenate([at_pick(score_slabs, idx) / denom * ROUTED_SCALE for idx in picks], axis=0)
    cnt_row = lax.dot_general(jnp.ones((GROUP_SIZE, tm), BF16), chosen_b, (((1,), (1,)), ((), ())),
                              preferred_element_type=F32)
    return pos, wts, pad(cnt_row).astype(jnp.int32)


def _mix_kernel(four_ref, att_ref, x_ref, mod_ref, gpost_ref, gpre_ref, wo_ref, wr_ref, rb_ref,
                wsg_ref, wsu_ref, wsd_ref, x1_ref, h2_ref, sh_ref, pos_ref, wts_ref, cp_ref):
    wr = wr_ref[...]
    wr_hi = wr.astype(BF16)
    wr_lo = (wr - wr_hi.astype(F32)).astype(BF16)
    for t in range(x_ref.shape[1] // MOE_TILE):
        rows = slice(t * MOE_TILE, (t + 1) * MOE_TILE)
        y = (jnp.dot(four_ref[0, rows, :], wo_ref[0:FOURIER_WIDTH, :], preferred_element_type=F32)
             + jnp.dot(att_ref[0, rows, :], wo_ref[FOURIER_WIDTH:, :], preferred_element_type=F32))
        x1 = x_ref[0, rows, :] + mod_ref[0, 2:3, :] * (_rms(y) * gpost_ref[...])
        x1_ref[0, rows, :] = x1
        h2 = _rms(x1) * gpre_ref[...]
        h2 = h2 * (1.0 + mod_ref[0, 4:5, :]) + mod_ref[0, 3:4, :]
        hb = h2.astype(BF16)
        h2_ref[0, rows, :] = hb
        a = jnp.dot(hb, wsg_ref[...], preferred_element_type=F32)
        u = jnp.dot(hb, wsu_ref[...], preferred_element_type=F32)
        hid = (a * jax.nn.sigmoid(a)) * u
        sh_ref[0, rows, :] = jnp.dot(hid.astype(BF16), wsd_ref[...], preferred_element_type=F32)
        h_lo = (h2 - hb.astype(F32)).astype(BF16)
        logits = (jnp.dot(hb, wr_hi, preferred_element_type=F32) + jnp.dot(hb, wr_lo, preferred_element_type=F32)
                  + jnp.dot(h_lo, wr_hi, preferred_element_type=F32))
        pos_ref[t], wts_ref[t], cp_ref[t] = _route(logits.T, rb_ref[...])


def _mix(four, att, x, mod, g_post, g_pre, w_out_b, w_router, rb_col, wsg, wsu, wsd):
    b, n, d = x.shape
    sub = MIX_TILES_PER_STEP if n % (MIX_TILES_PER_STEP * MOE_TILE) == 0 else 1
    tm = sub * MOE_TILE
    steps = n // tm
    tok = lambda w: pl.BlockSpec((1, tm, w), lambda bi, i: (bi, i, 0))
    full = lambda a: pl.BlockSpec(a.shape, lambda bi, i: (0,) * a.ndim)
    plan = lambda r, w: pl.BlockSpec((sub, r, w), lambda bi, i: (bi * steps + i, 0, 0))
    n_tiles = b * n // MOE_TILE
    return pl.pallas_call(
        _mix_kernel,
        out_shape=[jax.ShapeDtypeStruct((b, n, d), F32), jax.ShapeDtypeStruct((b, n, d), BF16),
                   jax.ShapeDtypeStruct((b, n, d), F32),
                   jax.ShapeDtypeStruct((n_tiles, TOP_K, MOE_TILE), jnp.int32),
                   jax.ShapeDtypeStruct((n_tiles, TOP_K, MOE_TILE), F32),
                   jax.ShapeDtypeStruct((n_tiles, GROUP_SIZE, N_EXPERTS), jnp.int32)],
        grid=(b, steps),
        in_specs=[tok(FOURIER_WIDTH), tok(DIFF_V_WIDTH), tok(d), pl.BlockSpec((1, N_MOD, d), lambda bi, i: (bi, 0, 0)),
                  full(g_post), full(g_pre), full(w_out_b), full(w_router), full(rb_col), full(wsg), full(wsu), full(wsd)],
        out_specs=[tok(d), tok(d), tok(d), plan(TOP_K, MOE_TILE), plan(TOP_K, MOE_TILE), plan(GROUP_SIZE, N_EXPERTS)],
        compiler_params=_cparams(("parallel", "parallel")),
        name="mix",
    )(four, att, x, mod, g_post, g_pre, w_out_b, w_router, rb_col, wsg, wsu, wsd)


def _moe_plan(cp):
    n_tiles = cp.shape[0]
    ends_loc = jnp.cumsum(cp, axis=1)
    seg_loc = ends_loc - cp
    reg = (jnp.sum(cp, axis=0) + MOE_BLOCK - 1) // MOE_BLOCK * MOE_BLOCK
    reg_end = jnp.cumsum(reg)
    seg_glob = (reg_end - reg)[None, :] + jnp.cumsum(cp, axis=0) - cp
    g_row = jnp.arange(LOC_ROWS // SEG, dtype=jnp.int32) * SEG
    e_of_g = jnp.sum((ends_loc[:, None, :] <= g_row[None, :, None]).astype(jnp.int32), axis=-1)
    owner = jnp.minimum(e_of_g, N_EXPERTS - 1)[:, :, None] == jnp.arange(N_EXPERTS, dtype=jnp.int32)[None, None, :]
    delta = jnp.sum(jnp.where(owner, (seg_glob - seg_loc)[:, None, :], 0), axis=-1)
    n_glob = _moe_rows(n_tiles * MOE_TILE)
    gdst = jnp.clip((delta + g_row[None, :]) // SEG, 0, n_glob // SEG - 1)
    ngran = ends_loc[:, -1] // SEG
    n_blocks = reg_end[-1] // MOE_BLOCK
    blk = jnp.minimum(jnp.arange(n_glob // MOE_BLOCK, dtype=jnp.int32), n_blocks - 1) * MOE_BLOCK
    blk_expert = jnp.minimum(jnp.sum((reg_end[None, :] <= blk[:, None]).astype(jnp.int32), axis=1), N_EXPERTS - 1)
    seg_tot = jnp.sum(cp, axis=0)
    k = jnp.arange(TAIL_SLOTS, dtype=jnp.int32)
    tail_gran = jnp.clip(((reg_end - reg + seg_tot) // SEG)[:, None] + k[None, :], 0, n_glob // SEG - 1)
    tail_ok = k[None, :] < ((reg - seg_tot) // SEG)[:, None]
    i32 = lambda a: a.astype(jnp.int32)
    return {"gdst": i32(gdst.reshape(-1)), "ngran": i32(ngran), "blk_expert": i32(blk_expert),
            "n_blocks": i32(n_blocks.reshape(1)), "tail_gran": i32(tail_gran.reshape(-1)),
            "tail_ok": i32(tail_ok.reshape(-1)), "n_tail": i32(jnp.sum(tail_ok).reshape(1))}


def _moe_rows(n_tok):
    rows = n_tok * TOP_K + (n_tok // MOE_TILE) * N_EXPERTS * (SEG - 1) + N_EXPERTS * (MOE_BLOCK - 1)
    return -(-rows // MOE_BLOCK) * MOE_BLOCK


def _dispatch_kernel(gdst_ref, ngran_ref, tail_gran_ref, tail_ok_ref, n_tail_ref, pos_ref, posn_ref, h_ref, xs_ref,
                     xloc_ref, p_ref, zero_ref, sem_ref):
    i = pl.program_id(0)
    last = pl.num_programs(0) - 1
    buf = lax.rem(i, 2)
    gl = LOC_ROWS // SEG

    def granule(b, g_loc, g_glob):
        return pltpu.make_async_copy(xloc_ref.at[b, pl.ds(g_loc * SEG, SEG), :],
                                     xs_ref.at[pl.ds(g_glob * SEG, SEG), :], sem_ref.at[b])

    def tail_granule(g_glob):
        return pltpu.make_async_copy(zero_ref, xs_ref.at[pl.ds(g_glob * SEG, SEG), :], sem_ref.at[2])

    def wait_tile(tile, b):
        lax.fori_loop(0, ngran_ref[tile], lambda g, c: (granule(b, 0, 0).wait(), c)[1], 0)

    @pl.when(i == 0)
    def _():
        zero_ref[...] = jnp.zeros(zero_ref.shape, BF16)

        def body(s, c):
            @pl.when(tail_ok_ref[s] == 1)
            def _():
                tail_granule(tail_gran_ref[s]).start()

            return c

        lax.fori_loop(0, N_EXPERTS * TAIL_SLOTS, body, 0)

    @pl.when(i >= 2)
    def _():
        wait_tile(i - 2, buf)

    def build_onehot(pos, slot):
        for rb in range(LOC_ROWS // LOC_BLOCK):
            r = lax.broadcasted_iota(jnp.int32, (LOC_BLOCK, MOE_TILE), 0) + rb * LOC_BLOCK
            r16 = r.astype(jnp.int16)
            pos16 = pos.astype(jnp.int16)
            onehot = jnp.zeros((LOC_BLOCK, MOE_TILE), BF16)
            for j in range(TOP_K):
                onehot = jnp.where(r16 == pos16[j:j + 1, :], jnp.ones_like(onehot), onehot)
            p_ref[slot, rb * LOC_BLOCK:(rb + 1) * LOC_BLOCK, :] = onehot

    @pl.when(i == 0)
    def _():
        build_onehot(pos_ref[0], 0)

    build_onehot(posn_ref[0], 1 - buf)
    h = h_ref[...]
    for rb in range(LOC_ROWS // LOC_BLOCK):
        rows = slice(rb * LOC_BLOCK, (rb + 1) * LOC_BLOCK)
        xloc_ref[buf, rows, :] = jnp.dot(p_ref[buf, rows, :], h, preferred_element_type=F32).astype(BF16)

    def issue(g, c):
        granule(buf, g, gdst_ref[i * gl + g]).start()
        return c

    lax.fori_loop(0, ngran_ref[i], issue, 0)

    @pl.when(i == last)
    def _():
        wait_tile(i, buf)

        @pl.when(i >= 1)
        def _():
            wait_tile(i - 1, 1 - buf)

        lax.fori_loop(0, n_tail_ref[0], lambda s, c: (tail_granule(0).wait(), c)[1], 0)


def _dispatch(h2, pos, plan):
    n_tok, d = h2.shape
    n_tiles = n_tok // MOE_TILE
    grid_spec = pltpu.PrefetchScalarGridSpec(
        num_scalar_prefetch=5,
        grid=(n_tiles,),
        in_specs=[pl.BlockSpec((1, TOP_K, MOE_TILE), lambda i, *_: (i, 0, 0)),
                  pl.BlockSpec((1, TOP_K, MOE_TILE), lambda i, *_: (jnp.minimum(i + 1, n_tiles - 1), 0, 0)),
                  pl.BlockSpec((MOE_TILE, d), lambda i, *_: (i, 0))],
        out_specs=pl.BlockSpec(memory_space=pl.ANY),
        scratch_shapes=[pltpu.VMEM((2, LOC_ROWS, d), BF16), pltpu.VMEM((2, LOC_ROWS, MOE_TILE), BF16),
                        pltpu.VMEM((SEG, d), BF16), pltpu.SemaphoreType.DMA((3,))],
    )
    return pl.pallas_call(
        _dispatch_kernel,
        out_shape=jax.ShapeDtypeStruct((_moe_rows(n_tok), d), BF16),
        grid_spec=grid_spec,
        compiler_params=_cparams(("arbitrary",)),
        name="moe_dispatch",
    )(plan["gdst"], plan["ngran"], plan["tail_gran"], plan["tail_ok"], plan["n_tail"], pos, pos, h2)


def _expert_kernel(be_ref, nb_ref, x_ref, wg_ref, wu_ref, wd_ref, y_ref):
    @pl.when(pl.program_id(0) < nb_ref[0])
    def _():
        x = x_ref[...]
        a = jnp.dot(x, wg_ref[0], preferred_element_type=F32)
        u = jnp.dot(x, wu_ref[0], preferred_element_type=F32)
        hid = (a * jax.nn.sigmoid(a)) * u
        y_ref[...] = jnp.dot(hid.astype(BF16), wd_ref[0], preferred_element_type=F32).astype(BF16)


def _experts(xs, blk_expert, n_blocks, wg, wu, wd):
    rows, d = xs.shape
    de = wg.shape[-1]
    live = lambda i, be, nb: jnp.maximum(jnp.minimum(i, nb[0] - 1), 0)
    grid_spec = pltpu.PrefetchScalarGridSpec(
        num_scalar_prefetch=2,
        grid=(rows // MOE_BLOCK,),
        in_specs=[pl.BlockSpec((MOE_BLOCK, d), lambda i, be, nb: (live(i, be, nb), 0)),
                  pl.BlockSpec((1, d, de), lambda i, be, nb: (be[i], 0, 0)),
                  pl.BlockSpec((1, d, de), lambda i, be, nb: (be[i], 0, 0)),
                  pl.BlockSpec((1, de, d), lambda i, be, nb: (be[i], 0, 0))],
        out_specs=pl.BlockSpec((MOE_BLOCK, d), lambda i, be, nb: (live(i, be, nb), 0)),
    )
    return pl.pallas_call(
        _expert_kernel,
        out_shape=jax.ShapeDtypeStruct((rows, d), BF16),
        grid_spec=grid_spec,
        compiler_params=_cparams(("arbitrary",)),
        name="moe_experts",
    )(blk_expert, n_blocks, xs, wg, wu, wd)


def _combine_kernel(gdst_ref, ngran_ref, ys_ref, pos_ref, wts_ref, posn_ref, wtsn_ref, sh_ref, x1_ref, mod_ref, g_ref,
                    o_ref, yloc_ref, w_ref, sem_ref):
    i = pl.program_id(0)
    last = pl.num_programs(0) - 1
    buf = lax.rem(i, 2)
    gl = LOC_ROWS // SEG

    def granule(b, g_loc, g_glob):
        return pltpu.make_async_copy(ys_ref.at[pl.ds(g_glob * SEG, SEG), :],
                                     yloc_ref.at[b, pl.ds(g_loc * SEG, SEG), :], sem_ref.at[b])

    def fetch(tile, b):
        def body(g, c):
            granule(b, g, gdst_ref[tile * gl + g]).start()
            return c

        lax.fori_loop(0, ngran_ref[tile], body, 0)

    @pl.when(i == 0)
    def _():
        yloc_ref[...] = jnp.zeros(yloc_ref.shape, BF16)
        fetch(0, 0)

    @pl.when(i < last)
    def _():
        fetch(i + 1, 1 - buf)

    lax.fori_loop(0, ngran_ref[i], lambda g, c: (granule(buf, 0, 0).wait(), c)[1], 0)

    def build_weights(pos, wts, slot):
        lane = lax.broadcasted_iota(jnp.int32, (MOE_TILE, LANES), 1)
        pos_b = [jnp.broadcast_to(pos[:, j:j + 1], (MOE_TILE, LANES)).astype(jnp.int16) for j in range(TOP_K)]
        wts_b = [jnp.broadcast_to(wts[:, j:j + 1], (MOE_TILE, LANES)).astype(BF16) for j in range(TOP_K)]
        for c in range(LOC_ROWS // LANES):
            target = (lane + c * LANES).astype(jnp.int16)
            w = jnp.zeros((MOE_TILE, LANES), BF16)
            for j in range(TOP_K):
                w = jnp.where(pos_b[j] == target, wts_b[j], w)
            w_ref[slot, :, c * LANES:(c + 1) * LANES] = w

    @pl.when(i == 0)
    def _():
        build_weights(pos_ref[...], wts_ref[...], 0)

    build_weights(posn_ref[...], wtsn_ref[...], 1 - buf)
    routed = jnp.dot(w_ref[buf], yloc_ref[buf], preferred_element_type=F32)
    o_ref[...] = x1_ref[...] + mod_ref[0, 5:6, :] * (_rms(routed + sh_ref[...]) * g_ref[...])


def _combine(ys, pos_t, wts_t, shared, x1, mod, g_post, gdst, ngran, tiles_per_batch):
    n_tok, d = x1.shape
    n_tiles = n_tok // MOE_TILE
    tok = lambda w: pl.BlockSpec((MOE_TILE, w), lambda i, *_: (i, 0))
    nxt = lambda w: pl.BlockSpec((MOE_TILE, w), lambda i, *_: (jnp.minimum(i + 1, n_tiles - 1), 0))
    grid_spec = pltpu.PrefetchScalarGridSpec(
        num_scalar_prefetch=2,
        grid=(n_tiles,),
        in_specs=[pl.BlockSpec(memory_space=pl.ANY), tok(TOP_K), tok(TOP_K), nxt(TOP_K), nxt(TOP_K), tok(d), tok(d),
                  pl.BlockSpec((1, N_MOD, d), lambda i, *_: (i // tiles_per_batch, 0, 0)),
                  pl.BlockSpec(g_post.shape, lambda i, *_: (0, 0))],
        out_specs=tok(d),
        scratch_shapes=[pltpu.VMEM((2, LOC_ROWS, d), BF16), pltpu.VMEM((2, MOE_TILE, LOC_ROWS), BF16),
                        pltpu.SemaphoreType.DMA((2,))],
    )
    return pl.pallas_call(
        _combine_kernel,
        out_shape=jax.ShapeDtypeStruct((n_tok, d), F32),
        grid_spec=grid_spec,
        compiler_params=_cparams(("arbitrary",)),
        name="moe_combine",
    )(gdst, ngran, ys, pos_t, wts_t, pos_t, wts_t, shared, x1, mod, g_post)


def kernel(x, c, ctx, c_ctx, w_mod, b_mod, g_attn_pre, g_attn_post, w_in, w_fourier, lambda_q1, lambda_k1,
           lambda_q2, lambda_k2, g_subln, w_out, g_ffn_pre, g_ffn_post, w_router, router_bias, w_gate, w_up,
           w_down, ws_gate, ws_up, ws_down):
    b, n, d = x.shape
    assert w_mod.shape[0] == 1, "single-layer kernel"
    assert n % (DFT_N1 * DFT_KGROUP) == 0 and b + 1 <= 8
    row = lambda a: a[0].reshape(1, -1)

    cond8 = jnp.zeros((8, d), F32).at[:b].set(c).at[b].set(c_ctx)
    mod = _ada(cond8, w_mod[0], b_mod[0]).reshape(8, N_MOD, d)

    w_in_b = w_in[0].astype(BF16)
    cc, sc = _channel_dft(n)
    wf = jnp.zeros((FOURIER_GROUPS, FOURIER_GROUP_DIM, FOURIER_GROUPS, FOURIER_GROUP_DIM), F32)
    for g in range(FOURIER_GROUPS):
        wf = wf.at[g, :, g, :].set(w_fourier[0, g])
    wf = wf.reshape(FOURIER_WIDTH, FOURIER_WIDTH).astype(BF16)

    tiles = _tiles(n, ctx.shape[1])
    u, v2, q, k_all, v_all = _prep(x, ctx, mod, b, row(g_attn_pre), w_in_b, _rope_tables(n), cc, sc, wf)

    m_u, m_v = _dft_stage1(n // DFT_N1)
    a2, b2 = _dft_stage2(n)
    four = _fourier(u, v2, m_u, m_v, a2, b2)

    lam_vecs = jnp.concatenate([lambda_q1, lambda_k1, lambda_q2, lambda_k2], axis=0).astype(F32)
    att = _attention(q, k_all, v_all, lam_vecs, row(g_subln), tiles["attn_q"], tiles["attn_kv"])

    x1, h2, shared, pos, wts, cp = _mix(four, att, x, mod, row(g_attn_post), row(g_ffn_pre), w_out[0].astype(BF16),
                                        w_router[0], router_bias[0].reshape(-1, 1), ws_gate[0].astype(BF16),
                                        ws_up[0].astype(BF16), ws_down[0].astype(BF16))
    plan = _moe_plan(cp[:, 0, :])
    xs = _dispatch(h2.reshape(b * n, d), pos, plan)
    ys = _experts(xs, plan["blk_expert"], plan["n_blocks"], w_gate[0].astype(BF16), w_up[0].astype(BF16),
                  w_down[0].astype(BF16))
    token_major = lambda a: a.transpose(0, 2, 1).reshape(b * n, TOP_K)
    out = _combine(ys, token_major(pos), token_major(wts), shared.reshape(b * n, d), x1.reshape(b * n, d), mod,
                   row(g_ffn_post), plan["gdst"], plan["ngran"], n // MOE_TILE)
    return out.reshape(b, n, d)
```

```python
import functools
import math

import numpy as np
import jax
import jax.numpy as jnp
from jax import lax
from jax.experimental import pallas as pl
from jax.experimental.pallas import tpu as pltpu

GRID_W = 64
N_MOD = 6
NORM_EPS = 1e-6
FOURIER_GROUPS = 4
FOURIER_GROUP_DIM = 64
FOURIER_WIDTH = FOURIER_GROUPS * FOURIER_GROUP_DIM
DIFF_HEADS = 6
DIFF_HEAD_DIM = 64
DIFF_VALUE_DIM = 2 * DIFF_HEAD_DIM
DIFF_QK_WIDTH = DIFF_HEADS * 2 * DIFF_HEAD_DIM
DIFF_V_WIDTH = DIFF_HEADS * DIFF_VALUE_DIM
ROPE_BASE = 10000.0
ROPE_AXIS_DIM = DIFF_HEAD_DIM // 2
N_EXPERTS = 64
TOP_K = 8
N_EXPERT_GROUPS = 8
TOP_K_GROUPS = 4
GROUP_SIZE = N_EXPERTS // N_EXPERT_GROUPS
ROUTED_SCALE = 2.5
LAM_INIT = 0.8 - 0.6 * math.exp(-0.3 * 0)
Q_SCALE = DIFF_HEAD_DIM ** -0.5

DFT_N1 = 64
DFT_KGROUP = 8
LANES = 128
VMEM_LIMIT = 56 * 1024 * 1024

BF16 = jnp.bfloat16
F32 = jnp.float32


PREP_TILE = 1024
MOE_TILE = 256
MIX_TILES_PER_STEP = 2
SEG = 16
MOE_BLOCK = 1024
DMA_UNROLL = 8
TAIL_SLOTS = MOE_BLOCK // SEG - 1
LOC_BLOCK = 512
LOC_ROWS = -(-(MOE_TILE * TOP_K + N_EXPERTS * (SEG - 1)) // LOC_BLOCK) * LOC_BLOCK
ALWAYS_USED_BLOCKS = MOE_TILE * TOP_K // LOC_BLOCK

MAX_UNROLLED_KV_CHUNKS = 12
ATTN_KV_TARGET = 1408


def _tiles(n, n_ctx):
    n_keys = n + n_ctx
    kv = max(t for t in range(LANES, n_keys + 1, LANES) if n_keys % t == 0 and t <= ATTN_KV_TARGET)
    return {"attn_q": min(n, 512), "attn_kv": kv}


def _cparams(sem):
    return pltpu.CompilerParams(dimension_semantics=sem, vmem_limit_bytes=VMEM_LIMIT)


def _bdot(a, b):
    return jnp.dot(a.astype(BF16), b.astype(BF16), preferred_element_type=F32)


def _rms(x):
    return x * lax.rsqrt(jnp.mean(x * x, axis=-1, keepdims=True) + NORM_EPS)


_TWO_PI_HI = float(np.float32(2.0 * np.pi))
_TWO_PI_LO = float(np.float32(2.0 * np.pi - _TWO_PI_HI))


def _cos_sin(phase, period):
    t = phase.astype(F32) / period
    ang = t * _TWO_PI_HI + t * _TWO_PI_LO
    return jnp.cos(ang), jnp.sin(ang)


def _rope_tables(n_tokens):
    rows = n_tokens // GRID_W
    row = jnp.broadcast_to(jnp.arange(rows)[:, None], (rows, GRID_W)).reshape(-1)
    col = jnp.broadcast_to(jnp.arange(GRID_W)[None, :], (rows, GRID_W)).reshape(-1)
    inv_freq = ROPE_BASE ** (-jnp.arange(0, ROPE_AXIS_DIM, 2, dtype=F32) / ROPE_AXIS_DIM)
    pos = jnp.stack([row, col], axis=-1).astype(F32)
    ang = pos[:, :, None] * inv_freq
    cos, sin = jnp.cos(ang), jnp.sin(ang)
    z = jnp.zeros_like(sin[:, 0])
    reps = LANES // DIFF_HEAD_DIM
    cos_l = jnp.tile(jnp.concatenate([cos[:, 0], cos[:, 0], cos[:, 1], cos[:, 1]], axis=-1), (1, reps))
    s_next = jnp.tile(jnp.concatenate([-sin[:, 0], z, -sin[:, 1], z], axis=-1), (1, reps))
    s_prev = jnp.tile(jnp.concatenate([z, sin[:, 0], z, sin[:, 1]], axis=-1), (1, reps))
    return cos_l, s_next, s_prev


def _channel_dft(n_tokens):
    c = jnp.arange(FOURIER_GROUP_DIM, dtype=jnp.int32)
    cs, sn = _cos_sin((c[:, None] * c[None, :]) % FOURIER_GROUP_DIM, FOURIER_GROUP_DIM)
    norm = lax.rsqrt(jnp.float32(n_tokens * FOURIER_GROUP_DIM))
    eye = jnp.eye(FOURIER_GROUPS, dtype=F32)
    return jnp.kron(eye, cs * norm).astype(BF16), jnp.kron(eye, sn * norm).astype(BF16)


def _dft_stage1(n2):
    k = jnp.arange(n2, dtype=jnp.int32)
    c, s = _cos_sin((k[:, None] * k[None, :]) % n2, n2)
    m_u = jnp.concatenate([c, -s], axis=0)
    m_v = jnp.concatenate([-s, -c], axis=0)
    return m_u.astype(BF16), m_v.astype(BF16)


def _dft_stage2(n_tokens):
    n2 = n_tokens // DFT_N1
    groups = n2 // DFT_KGROUP
    k1 = jnp.arange(DFT_N1, dtype=jnp.int32)[None, :, None, None]
    n1 = jnp.arange(DFT_N1, dtype=jnp.int32)[None, None, None, :]
    k2 = (jnp.arange(groups, dtype=jnp.int32)[:, None, None, None] * DFT_KGROUP
          + jnp.arange(DFT_KGROUP, dtype=jnp.int32)[None, None, :, None])
    c, s = _cos_sin((k1 * n1 * n2 + k2 * n1) % n_tokens, n_tokens)
    eye = jnp.eye(DFT_KGROUP, dtype=F32)[None, None, :, :, None]
    r = DFT_N1 * DFT_KGROUP
    a = (c[:, :, :, None, :] * eye).reshape(groups, r, r)
    b = (s[:, :, :, None, :] * eye).reshape(groups, r, r)
    return a.astype(BF16), b.astype(BF16)


def _ada_kernel(c_ref, w_ref, b_ref, o_ref):
    c = c_ref[...]
    s = c * jax.nn.sigmoid(c)
    o_ref[...] = jnp.dot(s, w_ref[...], preferred_element_type=F32) + b_ref[...]


def _ada(cond8, w_mod, b_mod):
    d = cond8.shape[1]
    return pl.pallas_call(
        _ada_kernel,
        out_shape=jax.ShapeDtypeStruct((8, N_MOD * d), F32),
        grid=(N_MOD,),
        in_specs=[pl.BlockSpec((8, d), lambda j: (0, 0)),
                  pl.BlockSpec((d, d), lambda j: (0, j)),
                  pl.BlockSpec((1, d), lambda j: (0, j))],
        out_specs=pl.BlockSpec((8, d), lambda j: (0, j)),
        compiler_params=_cparams(("arbitrary",)),
        name="ada",
    )(cond8, w_mod, b_mod.reshape(1, -1))


def _rope_chunk(ch, cos, s_next, s_prev):
    nxt = pltpu.roll(ch, LANES - ROPE_AXIS_DIM // 2, 1)
    prv = pltpu.roll(ch, ROPE_AXIS_DIM // 2, 1)
    return ch * cos + nxt * s_next + prv * s_prev


def _modulated(x, g_ref, mod_ref):
    h = _rms(x) * g_ref[...]
    return (h * (1.0 + mod_ref[0, 1:2, :]) + mod_ref[0, 0:1, :]).astype(BF16)


def _prep_kernel(x_ref, ctx_ref, mod_ref, modc_ref, g_ref, w_ref, cos_ref, sn_ref, sp_ref, cc_ref, sc_ref, wf_ref,
                 u_ref, v2_ref, q_ref, k_ref, v_ref):
    o1 = FOURIER_WIDTH
    o2 = o1 + DIFF_QK_WIDTH
    o3 = o2 + DIFF_QK_WIDTH
    step = pl.program_id(1)
    ctx_step = pl.num_programs(1) - 1
    n_ctx = ctx_ref.shape[1]

    @pl.when(step == ctx_step)
    def _():
        hb = _modulated(ctx_ref[0], g_ref, modc_ref)
        k_ref[0, 0:n_ctx, :] = jnp.dot(hb, w_ref[:, o2:o3], preferred_element_type=F32).astype(BF16)
        v_ref[0, 0:n_ctx, :] = jnp.dot(hb, w_ref[:, o3:], preferred_element_type=F32).astype(BF16)

    @pl.when(step < ctx_step)
    def _():
        hb = _modulated(x_ref[0], g_ref, mod_ref)
        f = jnp.dot(hb, w_ref[:, 0:o1], preferred_element_type=F32).astype(BF16)
        fc = jnp.dot(f, cc_ref[...], preferred_element_type=F32).astype(BF16)
        fs = jnp.dot(f, sc_ref[...], preferred_element_type=F32).astype(BF16)
        u_ref[0] = jnp.dot(fc, wf_ref[...], preferred_element_type=F32).astype(BF16)
        v2_ref[0] = jnp.dot(fs, wf_ref[...], preferred_element_type=F32).astype(BF16)
        cos, sn, sp = cos_ref[...], sn_ref[...], sp_ref[...]
        for j in range(DIFF_QK_WIDTH // LANES):
            lo = j * LANES
            qc = jnp.dot(hb, w_ref[:, o1 + lo:o1 + lo + LANES], preferred_element_type=F32)
            q_ref[0, :, lo:lo + LANES] = (_rope_chunk(qc, cos, sn, sp) * Q_SCALE).astype(BF16)
            kc = jnp.dot(hb, w_ref[:, o2 + lo:o2 + lo + LANES], preferred_element_type=F32)
            k_ref[0, :, lo:lo + LANES] = _rope_chunk(kc, cos, sn, sp).astype(BF16)
        v_ref[0] = jnp.dot(hb, w_ref[:, o3:], preferred_element_type=F32).astype(BF16)


def _prep(x, ctx, mod, ctx_row, g, w_in_b, rope, cc, sc, wf):
    b, n, d = x.shape
    n_ctx = ctx.shape[1]
    tm = min(n, PREP_TILE)
    assert n % tm == 0 and n_ctx <= tm
    steps = n // tm
    cos, sn, sp = rope
    lat = lambda i: jnp.minimum(i, steps - 1)
    tok = lambda w: pl.BlockSpec((1, tm, w), lambda bi, i: (bi, lat(i), 0))
    allk = lambda w: pl.BlockSpec((1, tm, w), lambda bi, i: (bi, i, 0))
    full = lambda a: pl.BlockSpec(a.shape, lambda bi, i: (0,) * a.ndim)
    tab = pl.BlockSpec((tm, LANES), lambda bi, i: (lat(i), 0))
    outs = [jax.ShapeDtypeStruct((b, n, w), BF16) for w in (FOURIER_WIDTH, FOURIER_WIDTH, DIFF_QK_WIDTH)]
    outs += [jax.ShapeDtypeStruct((b, n + n_ctx, w), BF16) for w in (DIFF_QK_WIDTH, DIFF_V_WIDTH)]
    return pl.pallas_call(
        _prep_kernel,
        out_shape=outs,
        grid=(b, steps + 1),
        in_specs=[tok(d), pl.BlockSpec((1, n_ctx, d), lambda bi, i: (bi, 0, 0)),
                  pl.BlockSpec((1, N_MOD, d), lambda bi, i: (bi, 0, 0)),
                  pl.BlockSpec((1, N_MOD, d), lambda bi, i: (ctx_row, 0, 0)),
                  full(g), full(w_in_b), tab, tab, tab, full(cc), full(sc), full(wf)],
        out_specs=[tok(FOURIER_WIDTH), tok(FOURIER_WIDTH), tok(DIFF_QK_WIDTH), allk(DIFF_QK_WIDTH), allk(DIFF_V_WIDTH)],
        compiler_params=_cparams(("parallel", "arbitrary")),
        name="prep",
    )(x, ctx, mod, mod, g, w_in_b, cos, sn, sp, cc, sc, wf)


def _dft1_kernel(u_ref, v_ref, mu_ref, mv_ref, yr_ref, yi_ref):
    n2 = u_ref.shape[1]
    y = (jnp.dot(mu_ref[...], u_ref[0], preferred_element_type=F32)
         + jnp.dot(mv_ref[...], v_ref[0], preferred_element_type=F32))
    yr_ref[0] = y[:n2].astype(BF16)
    yi_ref[0] = y[n2:].astype(BF16)


def _dft2_kernel(yr_ref, yi_ref, a_ref, b_ref, o_ref):
    y = (jnp.dot(a_ref[0], yr_ref[0], preferred_element_type=F32)
         + jnp.dot(b_ref[0], yi_ref[0], preferred_element_type=F32))
    o_ref[0] = y.reshape(DFT_N1, DFT_KGROUP, FOURIER_WIDTH).astype(BF16)


def _fourier(u, v, m_u, m_v, a2, b2):
    b, n, w = u.shape
    n2 = n // DFT_N1
    cols = DFT_N1 * w
    tn = min(cols, 2048)
    u2 = u.reshape(b, n2, cols)
    v2 = v.reshape(b, n2, cols)
    blk = pl.BlockSpec((1, n2, tn), lambda bi, j: (bi, 0, j))
    mat = pl.BlockSpec((2 * n2, n2), lambda bi, j: (0, 0))
    yr, yi = pl.pallas_call(
        _dft1_kernel,
        out_shape=[jax.ShapeDtypeStruct((b, n2, cols), BF16)] * 2,
        grid=(b, cols // tn),
        in_specs=[blk, blk, mat, mat],
        out_specs=[blk, blk],
        compiler_params=_cparams(("parallel", "parallel")),
        name="dft_stage1",
    )(u2, v2, m_u, m_v)
    yr = yr.reshape(b, n, w)
    yi = yi.reshape(b, n, w)
    rows = DFT_N1 * DFT_KGROUP
    groups = n2 // DFT_KGROUP
    yblk = pl.BlockSpec((1, rows, w), lambda bi, gi: (bi, gi, 0))
    mblk = pl.BlockSpec((1, rows, rows), lambda bi, gi: (gi, 0, 0))
    out = pl.pallas_call(
        _dft2_kernel,
        out_shape=jax.ShapeDtypeStruct((b, DFT_N1, n2, w), BF16),
        grid=(b, groups),
        in_specs=[yblk, yblk, mblk, mblk],
        out_specs=pl.BlockSpec((1, DFT_N1, DFT_KGROUP, w), lambda bi, gi: (bi, 0, gi, 0)),
        compiler_params=_cparams(("parallel", "parallel")),
        name="dft_stage2",
    )(yr, yi, a2, b2)
    return out.reshape(b, n, w)


def _attn_kernel(q_ref, k_ref, v_ref, lam_ref, g_ref, o_ref,
                 qs_ref, va_ref, m_ref, acc_ref, *, tk):
    tq = q_ref.shape[1]
    n_keys = k_ref.shape[1]

    @pl.when(pl.program_id(2) == 0)
    def _():
        va_ref[:, 0:LANES] = v_ref[0]
        va_ref[:, LANES:] = jnp.ones((n_keys, LANES), BF16)

    q = q_ref[0]
    lane = lax.broadcasted_iota(jnp.int32, (tq, LANES), 1)
    zero = jnp.zeros_like(q)
    qs_ref[0:tq, :] = jnp.where(lane < DIFF_HEAD_DIM, q, zero)
    qs_ref[tq:, :] = jnp.where(lane >= DIFF_HEAD_DIM, q, zero)
    m_ref[...] = jnp.full(m_ref.shape, -jnp.inf, F32)
    acc_ref[...] = jnp.zeros(acc_ref.shape, F32)

    def chunk(k, va):
        s = lax.dot_general(qs_ref[...], k, (((1,), (1,)), ((), ())), preferred_element_type=F32)
        cols = [s[:, c * LANES:(c + 1) * LANES] for c in range(k.shape[0] // LANES)]
        mx = cols[0]
        for col in cols[1:]:
            mx = jnp.maximum(mx, col)
        m_prev = m_ref[...]
        m_new = jnp.maximum(m_prev, jnp.max(mx, axis=1, keepdims=True))
        alpha = jnp.exp(m_prev - m_new)
        p = jnp.concatenate([jnp.exp(col - m_new).astype(BF16) for col in cols], axis=1)
        pv = jnp.dot(p, va, preferred_element_type=F32)
        acc_ref[...] = jnp.concatenate([alpha, alpha], axis=1) * acc_ref[...] + pv
        m_ref[...] = m_new

    n_chunks = n_keys // tk
    if n_chunks <= MAX_UNROLLED_KV_CHUNKS:
        for j in range(n_chunks):
            chunk(k_ref[0, j * tk:(j + 1) * tk, :], va_ref[j * tk:(j + 1) * tk, :])
    else:
        def body(j, carry):
            start = pl.multiple_of(j * tk, tk)
            chunk(k_ref[0, pl.ds(start, tk), :], va_ref[pl.ds(start, tk), :])
            return carry

        lax.fori_loop(0, n_chunks, body, 0)

    lam = (jnp.exp(jnp.sum(lam_ref[0:1, :] * lam_ref[1:2, :], axis=-1, keepdims=True))
           - jnp.exp(jnp.sum(lam_ref[2:3, :] * lam_ref[3:4, :], axis=-1, keepdims=True)) + LAM_INIT)
    o = (acc_ref[0:tq, 0:LANES] / acc_ref[0:tq, LANES:]
         - lam * (acc_ref[tq:, 0:LANES] / acc_ref[tq:, LANES:]))
    o_ref[0] = (_rms(o) * g_ref[...] * (1.0 - LAM_INIT)).astype(BF16)


def _attention(q, k_all, v_all, lam_vecs, g_subln, tq, tk):
    b, n, _ = q.shape
    n_keys = k_all.shape[1]
    assert n_keys % tk == 0
    qspec = pl.BlockSpec((1, tq, LANES), lambda bi, h, i: (bi, i, h))
    kspec = pl.BlockSpec((1, n_keys, LANES), lambda bi, h, i: (bi, 0, h))
    return pl.pallas_call(
        functools.partial(_attn_kernel, tk=tk),
        out_shape=jax.ShapeDtypeStruct((b, n, DIFF_V_WIDTH), BF16),
        grid=(b, DIFF_HEADS, n // tq),
        in_specs=[qspec, kspec, kspec,
                  pl.BlockSpec(lam_vecs.shape, lambda bi, h, i: (0, 0)),
                  pl.BlockSpec(g_subln.shape, lambda bi, h, i: (0, 0))],
        out_specs=qspec,
        scratch_shapes=[pltpu.VMEM((2 * tq, LANES), BF16), pltpu.VMEM((n_keys, 2 * LANES), BF16),
                        pltpu.VMEM((2 * tq, LANES), F32), pltpu.VMEM((2 * tq, 2 * LANES), F32)],
        compiler_params=_cparams(("parallel", "parallel", "arbitrary")),
        name="diff_attention",
    )(q, k_all, v_all, lam_vecs, g_subln)


def _route(logits_t, bias_col):
    tm = logits_t.shape[1]
    scores = jax.nn.sigmoid(logits_t)
    sel = scores + bias_col
    ninf = jnp.float32(-jnp.inf)
    i8 = lax.broadcasted_iota(jnp.int32, (GROUP_SIZE, tm), 0)
    slabs = [sel[GROUP_SIZE * g:GROUP_SIZE * (g + 1)] for g in range(N_EXPERT_GROUPS)]
    rows = []
    for slab in slabs:
        m1 = jnp.max(slab, axis=0, keepdims=True)
        i1 = jnp.min(jnp.where(slab == m1, i8, GROUP_SIZE), axis=0, keepdims=True)
        m2 = jnp.max(jnp.where(i8 == i1, ninf, slab), axis=0, keepdims=True)
        rows.append(m1 + m2)
    gs = jnp.concatenate(rows, axis=0)
    gmask = jnp.zeros(gs.shape, jnp.bool_)
    for _ in range(TOP_K_GROUPS):
        m = jnp.max(gs, axis=0, keepdims=True)
        i = jnp.min(jnp.where(gs == m, i8, N_EXPERT_GROUPS), axis=0, keepdims=True)
        hit = i8 == i
        gmask = jnp.logical_or(gmask, hit)
        gs = jnp.where(hit, ninf, gs)
    gmask_f = gmask.astype(F32)
    cand = [jnp.where(gmask_f[g:g + 1] > 0.5, slabs[g], ninf) for g in range(N_EXPERT_GROUPS)]
    chosen = [jnp.zeros((GROUP_SIZE, tm), jnp.bool_) for _ in range(N_EXPERT_GROUPS)]
    picks = []
    for _ in range(TOP_K):
        mm = cand[0]
        for g in range(1, N_EXPERT_GROUPS):
            mm = jnp.maximum(mm, cand[g])
        m = jnp.max(mm, axis=0, keepdims=True)
        ii = jnp.where(cand[0] == m, i8, N_EXPERTS)
        for g in range(1, N_EXPERT_GROUPS):
            ii = jnp.minimum(ii, jnp.where(cand[g] == m, i8 + GROUP_SIZE * g, N_EXPERTS))
        idx = jnp.min(ii, axis=0, keepdims=True)
        picks.append(idx)
        for g in range(N_EXPERT_GROUPS):
            hit = (i8 + GROUP_SIZE * g) == idx
            chosen[g] = jnp.logical_or(chosen[g], hit)
            cand[g] = jnp.where(hit, ninf, cand[g])
    score_slabs = [scores[GROUP_SIZE * g:GROUP_SIZE * (g + 1)] for g in range(N_EXPERT_GROUPS)]
    tot = jnp.where(chosen[0], score_slabs[0], 0.0)
    for g in range(1, N_EXPERT_GROUPS):
        tot = tot + jnp.where(chosen[g], score_slabs[g], 0.0)
    denom = jnp.sum(tot, axis=0, keepdims=True) + 1e-20

    chosen_f = jnp.concatenate([c.astype(F32) for c in chosen], axis=0)
    chosen_b = chosen_f.astype(BF16)
    earlier = (lax.broadcasted_iota(jnp.int32, (tm, tm), 0) < lax.broadcasted_iota(jnp.int32, (tm, tm), 1))
    rank = jnp.dot(chosen_b, earlier.astype(BF16), preferred_element_type=F32)
    pad = lambda cnt: jnp.floor((cnt + (SEG - 1)) * (1.0 / SEG)) * SEG
    cp_col = pad(jnp.sum(chosen_f, axis=1, keepdims=True))
    lower = (lax.broadcasted_iota(jnp.int32, (N_EXPERTS, N_EXPERTS), 1)
             < lax.broadcasted_iota(jnp.int32, (N_EXPERTS, N_EXPERTS), 0))
    seg_start = jnp.dot(lower.astype(BF16), jnp.broadcast_to(cp_col, (N_EXPERTS, LANES)).astype(BF16),
                        preferred_element_type=F32)[:, 0:1]
    base = seg_start + rank
    base_slabs = [base[GROUP_SIZE * g:GROUP_SIZE * (g + 1)] for g in range(N_EXPERT_GROUPS)]

    def at_pick(slabs_, idx):
        acc = jnp.zeros((1, tm), F32)
        for g in range(N_EXPERT_GROUPS):
            acc = acc + jnp.sum(jnp.where((i8 + GROUP_SIZE * g) == idx, slabs_[g], 0.0), axis=0, keepdims=True)
        return acc

    pos = jnp.concatenate([at_pick(base_slabs, idx) for idx in picks], axis=0).astype(jnp.int32)
    wts = jnp.concatenate([at_pick(score_slabs, idx) / denom * ROUTED_SCALE for idx in picks], axis=0)
    cnt_row = lax.dot_general(jnp.ones((GROUP_SIZE, tm), BF16), chosen_b, (((1,), (1,)), ((), ())),
                              preferred_element_type=F32)
    return pos, wts, pad(cnt_row).astype(jnp.int32)


def _mix_kernel(four_ref, att_ref, x_ref, mod_ref, gpost_ref, gpre_ref, wo_ref, wr_ref, rb_ref,
                wsg_ref, wsu_ref, wsd_ref, x1_ref, h2_ref, sh_ref, pos_ref, wts_ref, cp_ref):
    wr = wr_ref[...]
    wr_hi = wr.astype(BF16)
    wr_lo = (wr - wr_hi.astype(F32)).astype(BF16)
    for t in range(x_ref.shape[1] // MOE_TILE):
        rows = slice(t * MOE_TILE, (t + 1) * MOE_TILE)
        y = (jnp.dot(four_ref[0, rows, :], wo_ref[0:FOURIER_WIDTH, :], preferred_element_type=F32)
             + jnp.dot(att_ref[0, rows, :], wo_ref[FOURIER_WIDTH:, :], preferred_element_type=F32))
        x1 = x_ref[0, rows, :] + mod_ref[0, 2:3, :] * (_rms(y) * gpost_ref[...])
        x1_ref[0, rows, :] = x1
        h2 = _rms(x1) * gpre_ref[...]
        h2 = h2 * (1.0 + mod_ref[0, 4:5, :]) + mod_ref[0, 3:4, :]
        hb = h2.astype(BF16)
        h2_ref[0, rows, :] = hb
        a = jnp.dot(hb, wsg_ref[...], preferred_element_type=F32)
        u = jnp.dot(hb, wsu_ref[...], preferred_element_type=F32)
        hid = (a * jax.nn.sigmoid(a)) * u
        sh_ref[0, rows, :] = jnp.dot(hid.astype(BF16), wsd_ref[...], preferred_element_type=F32)
        h_lo = (h2 - hb.astype(F32)).astype(BF16)
        logits = (jnp.dot(hb, wr_hi, preferred_element_type=F32) + jnp.dot(hb, wr_lo, preferred_element_type=F32)
                  + jnp.dot(h_lo, wr_hi, preferred_element_type=F32))
        pos_ref[t], wts_ref[t], cp_ref[t] = _route(logits.T, rb_ref[...])


def _mix(four, att, x, mod, g_post, g_pre, w_out_b, w_router, rb_col, wsg, wsu, wsd):
    b, n, d = x.shape
    sub = MIX_TILES_PER_STEP if n % (MIX_TILES_PER_STEP * MOE_TILE) == 0 else 1
    tm = sub * MOE_TILE
    steps = n // tm
    tok = lambda w: pl.BlockSpec((1, tm, w), lambda bi, i: (bi, i, 0))
    full = lambda a: pl.BlockSpec(a.shape, lambda bi, i: (0,) * a.ndim)
    plan = lambda r, w: pl.BlockSpec((sub, r, w), lambda bi, i: (bi * steps + i, 0, 0))
    n_tiles = b * n // MOE_TILE
    return pl.pallas_call(
        _mix_kernel,
        out_shape=[jax.ShapeDtypeStruct((b, n, d), F32), jax.ShapeDtypeStruct((b, n, d), BF16),
                   jax.ShapeDtypeStruct((b, n, d), F32),
                   jax.ShapeDtypeStruct((n_tiles, TOP_K, MOE_TILE), jnp.int32),
                   jax.ShapeDtypeStruct((n_tiles, TOP_K, MOE_TILE), F32),
                   jax.ShapeDtypeStruct((n_tiles, GROUP_SIZE, N_EXPERTS), jnp.int32)],
        grid=(b, steps),
        in_specs=[tok(FOURIER_WIDTH), tok(DIFF_V_WIDTH), tok(d), pl.BlockSpec((1, N_MOD, d), lambda bi, i: (bi, 0, 0)),
                  full(g_post), full(g_pre), full(w_out_b), full(w_router), full(rb_col), full(wsg), full(wsu), full(wsd)],
        out_specs=[tok(d), tok(d), tok(d), plan(TOP_K, MOE_TILE), plan(TOP_K, MOE_TILE), plan(GROUP_SIZE, N_EXPERTS)],
        compiler_params=_cparams(("parallel", "parallel")),
        name="mix",
    )(four, att, x, mod, g_post, g_pre, w_out_b, w_router, rb_col, wsg, wsu, wsd)


def _moe_plan(cp):
    n_tiles = cp.shape[0]
    ends_loc = jnp.cumsum(cp, axis=1)
    seg_loc = ends_loc - cp
    reg = (jnp.sum(cp, axis=0) + MOE_BLOCK - 1) // MOE_BLOCK * MOE_BLOCK
    reg_end = jnp.cumsum(reg)
    seg_glob = (reg_end - reg)[None, :] + jnp.cumsum(cp, axis=0) - cp
    g_row = jnp.arange(LOC_ROWS // SEG, dtype=jnp.int32) * SEG
    e_of_g = jnp.sum((ends_loc[:, None, :] <= g_row[None, :, None]).astype(jnp.int32), axis=-1)
    owner = jnp.minimum(e_of_g, N_EXPERTS - 1)[:, :, None] == jnp.arange(N_EXPERTS, dtype=jnp.int32)[None, None, :]
    delta = jnp.sum(jnp.where(owner, (seg_glob - seg_loc)[:, None, :], 0), axis=-1)
    n_glob = _moe_rows(n_tiles * MOE_TILE)
    gdst = jnp.clip((delta + g_row[None, :]) // SEG, 0, n_glob // SEG - 1)
    ngran = ends_loc[:, -1] // SEG
    n_blocks = reg_end[-1] // MOE_BLOCK
    blk = jnp.minimum(jnp.arange(n_glob // MOE_BLOCK, dtype=jnp.int32), n_blocks - 1) * MOE_BLOCK
    blk_expert = jnp.minimum(jnp.sum((reg_end[None, :] <= blk[:, None]).astype(jnp.int32), axis=1), N_EXPERTS - 1)
    seg_tot = jnp.sum(cp, axis=0)
    k = jnp.arange(TAIL_SLOTS, dtype=jnp.int32)
    tail_gran = jnp.clip(((reg_end - reg + seg_tot) // SEG)[:, None] + k[None, :], 0, n_glob // SEG - 1)
    tail_ok = k[None, :] < ((reg - seg_tot) // SEG)[:, None]
    i32 = lambda a: a.astype(jnp.int32)
    return {"gdst": i32(gdst.reshape(-1)), "ngran": i32(ngran), "blk_expert": i32(blk_expert),
            "n_blocks": i32(n_blocks.reshape(1)), "tail_gran": i32(tail_gran.reshape(-1)),
            "tail_ok": i32(tail_ok.reshape(-1)), "n_tail": i32(jnp.sum(tail_ok).reshape(1))}


def _moe_rows(n_tok):
    rows = n_tok * TOP_K + (n_tok // MOE_TILE) * N_EXPERTS * (SEG - 1) + N_EXPERTS * (MOE_BLOCK - 1)
    return -(-rows // MOE_BLOCK) * MOE_BLOCK


def _for_granules(n, fn):
    groups = lax.shift_right_logical(n, DMA_UNROLL.bit_length() - 1)

    def group(k, c):
        for u in range(DMA_UNROLL):
            fn(k * DMA_UNROLL + u)
        return c

    lax.fori_loop(0, groups, group, 0)
    lax.fori_loop(groups * DMA_UNROLL, n, lambda g, c: (fn(g), c)[1], 0)


def _wait_granules(n, bulk, single):
    groups = lax.shift_right_logical(n, DMA_UNROLL.bit_length() - 1)
    lax.fori_loop(0, groups, lambda g, c: (bulk.wait(), c)[1], 0)
    lax.fori_loop(groups * DMA_UNROLL, n, lambda g, c: (single.wait(), c)[1], 0)


def _dispatch_kernel(gdst_ref, ngran_ref, tail_gran_ref, tail_ok_ref, n_tail_ref, pos_ref, posn_ref, h_ref, xs_ref,
                     xloc_ref, p_ref, zero_ref, sem_ref):
    i = pl.program_id(0)
    last = pl.num_programs(0) - 1
    buf = lax.rem(i, 2)
    gl = LOC_ROWS // SEG

    def granule(b, g_loc, g_glob):
        return pltpu.make_async_copy(xloc_ref.at[b, pl.ds(g_loc * SEG, SEG), :],
                                     xs_ref.at[pl.ds(g_glob * SEG, SEG), :], sem_ref.at[b])

    def tail_granule(g_glob):
        return pltpu.make_async_copy(zero_ref, xs_ref.at[pl.ds(g_glob * SEG, SEG), :], sem_ref.at[2])

    def wait_tile(tile, b):
        bulk = pltpu.make_async_copy(xloc_ref.at[b, pl.ds(0, DMA_UNROLL * SEG), :],
                                     xs_ref.at[pl.ds(0, DMA_UNROLL * SEG), :], sem_ref.at[b])
        _wait_granules(ngran_ref[tile], bulk, granule(b, 0, 0))

    @pl.when(i == 0)
    def _():
        zero_ref[...] = jnp.zeros(zero_ref.shape, BF16)

        def body(s, c):
            @pl.when(tail_ok_ref[s] == 1)
            def _():
                tail_granule(tail_gran_ref[s]).start()

            return c

        lax.fori_loop(0, N_EXPERTS * TAIL_SLOTS, body, 0)

    @pl.when(i >= 2)
    def _():
        wait_tile(i - 2, buf)

    def build_onehot(pos, slot):
        for rb in range(LOC_ROWS // LOC_BLOCK):
            r = lax.broadcasted_iota(jnp.int32, (LOC_BLOCK, MOE_TILE), 0) + rb * LOC_BLOCK
            r16 = r.astype(jnp.int16)
            pos16 = pos.astype(jnp.int16)
            onehot = jnp.zeros((LOC_BLOCK, MOE_TILE), BF16)
            for j in range(TOP_K):
                onehot = jnp.where(r16 == pos16[j:j + 1, :], jnp.ones_like(onehot), onehot)
            p_ref[slot, rb * LOC_BLOCK:(rb + 1) * LOC_BLOCK, :] = onehot

    @pl.when(i == 0)
    def _():
        build_onehot(pos_ref[0], 0)

    build_onehot(posn_ref[0], 1 - buf)
    h = h_ref[...]
    for rb in range(LOC_ROWS // LOC_BLOCK):
        rows = slice(rb * LOC_BLOCK, (rb + 1) * LOC_BLOCK)
        xloc_ref[buf, rows, :] = jnp.dot(p_ref[buf, rows, :], h, preferred_element_type=F32).astype(BF16)

    _for_granules(ngran_ref[i], lambda g: granule(buf, g, gdst_ref[i * gl + g]).start())

    @pl.when(i == last)
    def _():
        wait_tile(i, buf)

        @pl.when(i >= 1)
        def _():
            wait_tile(i - 1, 1 - buf)

        lax.fori_loop(0, n_tail_ref[0], lambda s, c: (tail_granule(0).wait(), c)[1], 0)


def _dispatch(h2, pos, plan):
    n_tok, d = h2.shape
    n_tiles = n_tok // MOE_TILE
    grid_spec = pltpu.PrefetchScalarGridSpec(
        num_scalar_prefetch=5,
        grid=(n_tiles,),
        in_specs=[pl.BlockSpec((1, TOP_K, MOE_TILE), lambda i, *_: (i, 0, 0)),
                  pl.BlockSpec((1, TOP_K, MOE_TILE), lambda i, *_: (jnp.minimum(i + 1, n_tiles - 1), 0, 0)),
                  pl.BlockSpec((MOE_TILE, d), lambda i, *_: (i, 0))],
        out_specs=pl.BlockSpec(memory_space=pl.ANY),
        scratch_shapes=[pltpu.VMEM((2, LOC_ROWS, d), BF16), pltpu.VMEM((2, LOC_ROWS, MOE_TILE), BF16),
                        pltpu.VMEM((SEG, d), BF16), pltpu.SemaphoreType.DMA((3,))],
    )
    return pl.pallas_call(
        _dispatch_kernel,
        out_shape=jax.ShapeDtypeStruct((_moe_rows(n_tok), d), BF16),
        grid_spec=grid_spec,
        compiler_params=_cparams(("arbitrary",)),
        name="moe_dispatch",
    )(plan["gdst"], plan["ngran"], plan["tail_gran"], plan["tail_ok"], plan["n_tail"], pos, pos, h2)


def _expert_kernel(be_ref, nb_ref, x_ref, wg_ref, wu_ref, wd_ref, y_ref, wgb_ref, wub_ref, wdb_ref):
    i = pl.program_id(0)

    @pl.when(jnp.logical_or(i == 0, be_ref[i] != be_ref[jnp.maximum(i - 1, 0)]))
    def _():
        wgb_ref[...] = wg_ref[0].astype(BF16)
        wub_ref[...] = wu_ref[0].astype(BF16)
        wdb_ref[...] = wd_ref[0].astype(BF16)

    @pl.when(i < nb_ref[0])
    def _():
        x = x_ref[...]
        a = jnp.dot(x, wgb_ref[...], preferred_element_type=F32)
        u = jnp.dot(x, wub_ref[...], preferred_element_type=F32)
        hid = (a * jax.nn.sigmoid(a)) * u
        y_ref[...] = jnp.dot(hid.astype(BF16), wdb_ref[...], preferred_element_type=F32).astype(BF16)


def _experts(xs, blk_expert, n_blocks, wg, wu, wd):
    rows, d = xs.shape
    de = wg.shape[-1]
    live = lambda i, be, nb: jnp.maximum(jnp.minimum(i, nb[0] - 1), 0)
    grid_spec = pltpu.PrefetchScalarGridSpec(
        num_scalar_prefetch=2,
        grid=(rows // MOE_BLOCK,),
        in_specs=[pl.BlockSpec((MOE_BLOCK, d), lambda i, be, nb: (live(i, be, nb), 0)),
                  pl.BlockSpec((1, d, de), lambda i, be, nb: (be[i], 0, 0)),
                  pl.BlockSpec((1, d, de), lambda i, be, nb: (be[i], 0, 0)),
                  pl.BlockSpec((1, de, d), lambda i, be, nb: (be[i], 0, 0))],
        out_specs=pl.BlockSpec((MOE_BLOCK, d), lambda i, be, nb: (live(i, be, nb), 0)),
        scratch_shapes=[pltpu.VMEM((d, de), BF16), pltpu.VMEM((d, de), BF16), pltpu.VMEM((de, d), BF16)],
    )
    return pl.pallas_call(
        _expert_kernel,
        out_shape=jax.ShapeDtypeStruct((rows, d), BF16),
        grid_spec=grid_spec,
        compiler_params=_cparams(("arbitrary",)),
        name="moe_experts",
    )(blk_expert, n_blocks, xs, wg, wu, wd)


def _combine_kernel(gdst_ref, ngran_ref, ys_ref, pos_ref, wts_ref, posn_ref, wtsn_ref, sh_ref, x1_ref, mod_ref, g_ref,
                    o_ref, yloc_ref, w_ref, sem_ref):
    i = pl.program_id(0)
    last = pl.num_programs(0) - 1
    buf = lax.rem(i, 2)
    gl = LOC_ROWS // SEG

    def granule(b, g_loc, g_glob):
        return pltpu.make_async_copy(ys_ref.at[pl.ds(g_glob * SEG, SEG), :],
                                     yloc_ref.at[b, pl.ds(g_loc * SEG, SEG), :], sem_ref.at[b])

    def fetch(tile, b):
        _for_granules(ngran_ref[tile], lambda g: granule(b, g, gdst_ref[tile * gl + g]).start())

    @pl.when(i == 0)
    def _():
        yloc_ref[...] = jnp.zeros(yloc_ref.shape, BF16)
        fetch(0, 0)

    @pl.when(i < last)
    def _():
        fetch(i + 1, 1 - buf)

    bulk = pltpu.make_async_copy(ys_ref.at[pl.ds(0, DMA_UNROLL * SEG), :],
                                 yloc_ref.at[buf, pl.ds(0, DMA_UNROLL * SEG), :], sem_ref.at[buf])
    _wait_granules(ngran_ref[i], bulk, granule(buf, 0, 0))

    def build_weights(pos, wts, slot):
        lane = lax.broadcasted_iota(jnp.int32, (MOE_TILE, LANES), 1)
        pos_b = [jnp.broadcast_to(pos[:, j:j + 1], (MOE_TILE, LANES)).astype(jnp.int16) for j in range(TOP_K)]
        wts_b = [jnp.broadcast_to(wts[:, j:j + 1], (MOE_TILE, LANES)).astype(BF16) for j in range(TOP_K)]
        for c in range(LOC_ROWS // LANES):
            target = (lane + c * LANES).astype(jnp.int16)
            w = jnp.zeros((MOE_TILE, LANES), BF16)
            for j in range(TOP_K):
                w = jnp.where(pos_b[j] == target, wts_b[j], w)
            w_ref[slot, :, c * LANES:(c + 1) * LANES] = w

    @pl.when(i == 0)
    def _():
        build_weights(pos_ref[...], wts_ref[...], 0)

    build_weights(posn_ref[...], wtsn_ref[...], 1 - buf)
    routed = jnp.dot(w_ref[buf], yloc_ref[buf], preferred_element_type=F32)
    o_ref[...] = x1_ref[...] + mod_ref[0, 5:6, :] * (_rms(routed + sh_ref[...]) * g_ref[...])


def _combine(ys, pos_t, wts_t, shared, x1, mod, g_post, gdst, ngran, tiles_per_batch):
    n_tok, d = x1.shape
    n_tiles = n_tok // MOE_TILE
    tok = lambda w: pl.BlockSpec((MOE_TILE, w), lambda i, *_: (i, 0))
    nxt = lambda w: pl.BlockSpec((MOE_TILE, w), lambda i, *_: (jnp.minimum(i + 1, n_tiles - 1), 0))
    grid_spec = pltpu.PrefetchScalarGridSpec(
        num_scalar_prefetch=2,
        grid=(n_tiles,),
        in_specs=[pl.BlockSpec(memory_space=pl.ANY), tok(TOP_K), tok(TOP_K), nxt(TOP_K), nxt(TOP_K), tok(d), tok(d),
                  pl.BlockSpec((1, N_MOD, d), lambda i, *_: (i // tiles_per_batch, 0, 0)),
                  pl.BlockSpec(g_post.shape, lambda i, *_: (0, 0))],
        out_specs=tok(d),
        scratch_shapes=[pltpu.VMEM((2, LOC_ROWS, d), BF16), pltpu.VMEM((2, MOE_TILE, LOC_ROWS), BF16),
                        pltpu.SemaphoreType.DMA((2,))],
    )
    return pl.pallas_call(
        _combine_kernel,
        out_shape=jax.ShapeDtypeStruct((n_tok, d), F32),
        grid_spec=grid_spec,
        compiler_params=_cparams(("arbitrary",)),
        name="moe_combine",
    )(gdst, ngran, ys, pos_t, wts_t, pos_t, wts_t, shared, x1, mod, g_post)


def kernel(x, c, ctx, c_ctx, w_mod, b_mod, g_attn_pre, g_attn_post, w_in, w_fourier, lambda_q1, lambda_k1,
           lambda_q2, lambda_k2, g_subln, w_out, g_ffn_pre, g_ffn_post, w_router, router_bias, w_gate, w_up,
           w_down, ws_gate, ws_up, ws_down):
    b, n, d = x.shape
    assert w_mod.shape[0] == 1, "single-layer kernel"
    assert n % (DFT_N1 * DFT_KGROUP) == 0 and b + 1 <= 8
    row = lambda a: a[0].reshape(1, -1)

    cond8 = jnp.zeros((8, d), F32).at[:b].set(c).at[b].set(c_ctx)
    mod = _ada(cond8, w_mod[0], b_mod[0]).reshape(8, N_MOD, d)

    w_in_b = w_in[0].astype(BF16)
    cc, sc = _channel_dft(n)
    wf = jnp.zeros((FOURIER_GROUPS, FOURIER_GROUP_DIM, FOURIER_GROUPS, FOURIER_GROUP_DIM), F32)
    for g in range(FOURIER_GROUPS):
        wf = wf.at[g, :, g, :].set(w_fourier[0, g])
    wf = wf.reshape(FOURIER_WIDTH, FOURIER_WIDTH).astype(BF16)

    tiles = _tiles(n, ctx.shape[1])
    u, v2, q, k_all, v_all = _prep(x, ctx, mod, b, row(g_attn_pre), w_in_b, _rope_tables(n), cc, sc, wf)

    m_u, m_v = _dft_stage1(n // DFT_N1)
    a2, b2 = _dft_stage2(n)
    four = _fourier(u, v2, m_u, m_v, a2, b2)

    lam_vecs = jnp.concatenate([lambda_q1, lambda_k1, lambda_q2, lambda_k2], axis=0).astype(F32)
    att = _attention(q, k_all, v_all, lam_vecs, row(g_subln), tiles["attn_q"], tiles["attn_kv"])

    x1, h2, shared, pos, wts, cp = _mix(four, att, x, mod, row(g_attn_post), row(g_ffn_pre), w_out[0].astype(BF16),
                                        w_router[0], router_bias[0].reshape(-1, 1), ws_gate[0].astype(BF16),
                                        ws_up[0].astype(BF16), ws_down[0].astype(BF16))
    plan = _moe_plan(cp[:, 0, :])
    xs = _dispatch(h2.reshape(b * n, d), pos, plan)
    ys = _experts(xs, plan["blk_expert"], plan["n_blocks"], w_gate[0], w_up[0], w_down[0])
    token_major = lambda a: a.transpose(0, 2, 1).reshape(b * n, TOP_K)
    out = _combine(ys, token_major(pos), token_major(wts), shared.reshape(b * n, d), x1.reshape(b * n, d), mod,
                   row(g_ffn_post), plan["gdst"], plan["ngran"], n // MOE_TILE)
    return out.reshape(b, n, d)
```

```python
import functools
import math

import numpy as np
import jax
import jax.numpy as jnp
from jax import lax
from jax.experimental import pallas as pl
from jax.experimental.pallas import tpu as pltpu

GRID_W = 64
N_MOD = 6
NORM_EPS = 1e-6
FOURIER_GROUPS = 4
FOURIER_GROUP_DIM = 64
FOURIER_WIDTH = FOURIER_GROUPS * FOURIER_GROUP_DIM
DIFF_HEADS = 6
DIFF_HEAD_DIM = 64
DIFF_VALUE_DIM = 2 * DIFF_HEAD_DIM
DIFF_QK_WIDTH = DIFF_HEADS * 2 * DIFF_HEAD_DIM
DIFF_V_WIDTH = DIFF_HEADS * DIFF_VALUE_DIM
ROPE_BASE = 10000.0
ROPE_AXIS_DIM = DIFF_HEAD_DIM // 2
N_EXPERTS = 64
TOP_K = 8
N_EXPERT_GROUPS = 8
TOP_K_GROUPS = 4
GROUP_SIZE = N_EXPERTS // N_EXPERT_GROUPS
ROUTED_SCALE = 2.5
LAM_INIT = 0.8 - 0.6 * math.exp(-0.3 * 0)
Q_SCALE = DIFF_HEAD_DIM ** -0.5

DFT_N1 = 64
DFT_KGROUP = 8
LANES = 128
MXU_WIDTH = 256
VMEM_LIMIT = 56 * 1024 * 1024

BF16 = jnp.bfloat16
F32 = jnp.float32


PREP_TILE = 1024
MOE_TILE = 256
MIX_TILES_PER_STEP = 4
SEG = 16
MOE_BLOCK = 1024
DMA_UNROLL = 8
TAIL_SLOTS = MOE_BLOCK // SEG - 1
LOC_BLOCK = 512
LOC_ROWS = -(-(MOE_TILE * TOP_K + N_EXPERTS * (SEG - 1)) // LOC_BLOCK) * LOC_BLOCK
ALWAYS_USED_BLOCKS = MOE_TILE * TOP_K // LOC_BLOCK

ATTN_Q_TILES_PER_STEP = 2
ATTN_KV_CHUNK = 1280


def _tiles(n, n_ctx):
    n_keys = n + n_ctx
    assert n_keys % LANES == 0
    full, rest = divmod(n_keys, ATTN_KV_CHUNK)
    kv_chunks = (ATTN_KV_CHUNK,) * full + ((rest,) if rest else ())
    tq = min(n, 512)
    q_tiles = ATTN_Q_TILES_PER_STEP if n % (ATTN_Q_TILES_PER_STEP * tq) == 0 else 1
    return {"attn_q": tq, "attn_kv": kv_chunks, "attn_q_tiles": q_tiles}


def _cparams(sem):
    return pltpu.CompilerParams(dimension_semantics=sem, vmem_limit_bytes=VMEM_LIMIT)


def _bdot(a, b):
    return jnp.dot(a.astype(BF16), b.astype(BF16), preferred_element_type=F32)


def _rms(x):
    return x * lax.rsqrt(jnp.mean(x * x, axis=-1, keepdims=True) + NORM_EPS)


_TWO_PI_HI = float(np.float32(2.0 * np.pi))
_TWO_PI_LO = float(np.float32(2.0 * np.pi - _TWO_PI_HI))


def _cos_sin(phase, period):
    t = phase.astype(F32) / period
    ang = t * _TWO_PI_HI + t * _TWO_PI_LO
    return jnp.cos(ang), jnp.sin(ang)


def _rope_tables(n_tokens):
    rows = n_tokens // GRID_W
    row = jnp.broadcast_to(jnp.arange(rows)[:, None], (rows, GRID_W)).reshape(-1)
    col = jnp.broadcast_to(jnp.arange(GRID_W)[None, :], (rows, GRID_W)).reshape(-1)
    inv_freq = ROPE_BASE ** (-jnp.arange(0, ROPE_AXIS_DIM, 2, dtype=F32) / ROPE_AXIS_DIM)
    pos = jnp.stack([row, col], axis=-1).astype(F32)
    ang = pos[:, :, None] * inv_freq
    cos, sin = jnp.cos(ang), jnp.sin(ang)
    z = jnp.zeros_like(sin[:, 0])
    reps = LANES // DIFF_HEAD_DIM
    cos_l = jnp.tile(jnp.concatenate([cos[:, 0], cos[:, 0], cos[:, 1], cos[:, 1]], axis=-1), (1, reps))
    s_next = jnp.tile(jnp.concatenate([-sin[:, 0], z, -sin[:, 1], z], axis=-1), (1, reps))
    s_prev = jnp.tile(jnp.concatenate([z, sin[:, 0], z, sin[:, 1]], axis=-1), (1, reps))
    return cos_l, s_next, s_prev


def _channel_dft(n_tokens):
    c = jnp.arange(FOURIER_GROUP_DIM, dtype=jnp.int32)
    cs, sn = _cos_sin((c[:, None] * c[None, :]) % FOURIER_GROUP_DIM, FOURIER_GROUP_DIM)
    norm = lax.rsqrt(jnp.float32(n_tokens * FOURIER_GROUP_DIM))
    eye = jnp.eye(FOURIER_GROUPS, dtype=F32)
    return jnp.kron(eye, cs * norm).astype(BF16), jnp.kron(eye, sn * norm).astype(BF16)


def _dft_stage1(n2):
    k = jnp.arange(n2, dtype=jnp.int32)
    c, s = _cos_sin((k[:, None] * k[None, :]) % n2, n2)
    m_u = jnp.concatenate([c, -s], axis=0)
    m_v = jnp.concatenate([-s, -c], axis=0)
    return m_u.astype(BF16), m_v.astype(BF16)


def _dft_stage2(n_tokens):
    n2 = n_tokens // DFT_N1
    groups = n2 // DFT_KGROUP
    k1 = jnp.arange(DFT_N1, dtype=jnp.int32)[None, :, None, None]
    n1 = jnp.arange(DFT_N1, dtype=jnp.int32)[None, None, None, :]
    k2 = (jnp.arange(groups, dtype=jnp.int32)[:, None, None, None] * DFT_KGROUP
          + jnp.arange(DFT_KGROUP, dtype=jnp.int32)[None, None, :, None])
    c, s = _cos_sin((k1 * n1 * n2 + k2 * n1) % n_tokens, n_tokens)
    eye = jnp.eye(DFT_KGROUP, dtype=F32)[None, None, :, :, None]
    r = DFT_N1 * DFT_KGROUP
    a = (c[:, :, :, None, :] * eye).reshape(groups, r, r)
    b = (s[:, :, :, None, :] * eye).reshape(groups, r, r)
    return a.astype(BF16), b.astype(BF16)


def _ada_kernel(c_ref, w_ref, b_ref, o_ref):
    c = c_ref[...]
    s = c * jax.nn.sigmoid(c)
    o_ref[...] = jnp.dot(s, w_ref[...], preferred_element_type=F32) + b_ref[...]


def _ada(cond8, w_mod, b_mod):
    d = cond8.shape[1]
    return pl.pallas_call(
        _ada_kernel,
        out_shape=jax.ShapeDtypeStruct((8, N_MOD * d), F32),
        grid=(N_MOD,),
        in_specs=[pl.BlockSpec((8, d), lambda j: (0, 0)),
                  pl.BlockSpec((d, d), lambda j: (0, j)),
                  pl.BlockSpec((1, d), lambda j: (0, j))],
        out_specs=pl.BlockSpec((8, d), lambda j: (0, j)),
        compiler_params=_cparams(("arbitrary",)),
        name="ada",
    )(cond8, w_mod, b_mod.reshape(1, -1))


def _rope_chunk(ch, cos, s_next, s_prev):
    nxt = pltpu.roll(ch, LANES - ROPE_AXIS_DIM // 2, 1)
    prv = pltpu.roll(ch, ROPE_AXIS_DIM // 2, 1)
    return ch * cos + nxt * s_next + prv * s_prev


def _modulated(x, g_ref, mod_ref):
    h = _rms(x) * g_ref[...]
    return (h * (1.0 + mod_ref[0, 1:2, :]) + mod_ref[0, 0:1, :]).astype(BF16)


def _prep_kernel(x_ref, ctx_ref, mod_ref, modc_ref, g_ref, w_ref, cos_ref, sn_ref, sp_ref, cc_ref, sc_ref, wf_ref,
                 u_ref, v2_ref, q_ref, k_ref, v_ref):
    o1 = FOURIER_WIDTH
    o2 = o1 + DIFF_QK_WIDTH
    o3 = o2 + DIFF_QK_WIDTH
    step = pl.program_id(1)
    ctx_step = pl.num_programs(1) - 1
    n_ctx = ctx_ref.shape[1]

    @pl.when(step == ctx_step)
    def _():
        hb = _modulated(ctx_ref[0], g_ref, modc_ref)
        k_ref[0, 0:n_ctx, :] = jnp.dot(hb, w_ref[:, o2:o3], preferred_element_type=F32).astype(BF16)
        v_ref[0, 0:n_ctx, :] = jnp.dot(hb, w_ref[:, o3:], preferred_element_type=F32).astype(BF16)

    @pl.when(step < ctx_step)
    def _():
        hb = _modulated(x_ref[0], g_ref, mod_ref)
        f = jnp.dot(hb, w_ref[:, 0:o1], preferred_element_type=F32).astype(BF16)
        fc = jnp.dot(f, cc_ref[...], preferred_element_type=F32).astype(BF16)
        fs = jnp.dot(f, sc_ref[...], preferred_element_type=F32).astype(BF16)
        u_ref[0] = jnp.dot(fc, wf_ref[...], preferred_element_type=F32).astype(BF16)
        v2_ref[0] = jnp.dot(fs, wf_ref[...], preferred_element_type=F32).astype(BF16)
        cos, sn, sp = cos_ref[...], sn_ref[...], sp_ref[...]
        for j in range(DIFF_QK_WIDTH // MXU_WIDTH):
            lo = j * MXU_WIDTH
            qc = jnp.dot(hb, w_ref[:, o1 + lo:o1 + lo + MXU_WIDTH], preferred_element_type=F32)
            kc = jnp.dot(hb, w_ref[:, o2 + lo:o2 + lo + MXU_WIDTH], preferred_element_type=F32)
            for c in range(MXU_WIDTH // LANES):
                cols = slice(c * LANES, (c + 1) * LANES)
                out = slice(lo + c * LANES, lo + (c + 1) * LANES)
                q_ref[0, :, out] = (_rope_chunk(qc[:, cols], cos, sn, sp) * Q_SCALE).astype(BF16)
                k_ref[0, :, out] = _rope_chunk(kc[:, cols], cos, sn, sp).astype(BF16)
        v_ref[0] = jnp.dot(hb, w_ref[:, o3:], preferred_element_type=F32).astype(BF16)


def _prep(x, ctx, mod, ctx_row, g, w_in_b, rope, cc, sc, wf):
    b, n, d = x.shape
    n_ctx = ctx.shape[1]
    tm = min(n, PREP_TILE)
    assert n % tm == 0 and n_ctx <= tm
    steps = n // tm
    cos, sn, sp = rope
    lat = lambda i: jnp.minimum(i, steps - 1)
    tok = lambda w: pl.BlockSpec((1, tm, w), lambda bi, i: (bi, lat(i), 0))
    allk = lambda w: pl.BlockSpec((1, tm, w), lambda bi, i: (bi, i, 0))
    full = lambda a: pl.BlockSpec(a.shape, lambda bi, i: (0,) * a.ndim)
    tab = pl.BlockSpec((tm, LANES), lambda bi, i: (lat(i), 0))
    outs = [jax.ShapeDtypeStruct((b, n, w), BF16) for w in (FOURIER_WIDTH, FOURIER_WIDTH, DIFF_QK_WIDTH)]
    outs += [jax.ShapeDtypeStruct((b, n + n_ctx, w), BF16) for w in (DIFF_QK_WIDTH, DIFF_V_WIDTH)]
    return pl.pallas_call(
        _prep_kernel,
        out_shape=outs,
        grid=(b, steps + 1),
        in_specs=[tok(d), pl.BlockSpec((1, n_ctx, d), lambda bi, i: (bi, 0, 0)),
                  pl.BlockSpec((1, N_MOD, d), lambda bi, i: (bi, 0, 0)),
                  pl.BlockSpec((1, N_MOD, d), lambda bi, i: (ctx_row, 0, 0)),
                  full(g), full(w_in_b), tab, tab, tab, full(cc), full(sc), full(wf)],
        out_specs=[tok(FOURIER_WIDTH), tok(FOURIER_WIDTH), tok(DIFF_QK_WIDTH), allk(DIFF_QK_WIDTH), allk(DIFF_V_WIDTH)],
        compiler_params=_cparams(("parallel", "arbitrary")),
        name="prep",
    )(x, ctx, mod, mod, g, w_in_b, cos, sn, sp, cc, sc, wf)


def _dft1_kernel(u_ref, v_ref, mu_ref, mv_ref, yr_ref, yi_ref):
    n2 = u_ref.shape[1]
    y = (jnp.dot(mu_ref[...], u_ref[0], preferred_element_type=F32)
         + jnp.dot(mv_ref[...], v_ref[0], preferred_element_type=F32))
    yr_ref[0] = y[:n2].astype(BF16)
    yi_ref[0] = y[n2:].astype(BF16)


def _dft2_kernel(yr_ref, yi_ref, a_ref, b_ref, o_ref):
    y = (jnp.dot(a_ref[0], yr_ref[0], preferred_element_type=F32)
         + jnp.dot(b_ref[0], yi_ref[0], preferred_element_type=F32))
    o_ref[0] = y.reshape(DFT_N1, DFT_KGROUP, FOURIER_WIDTH).astype(BF16)


def _fourier(u, v, m_u, m_v, a2, b2):
    b, n, w = u.shape
    n2 = n // DFT_N1
    cols = DFT_N1 * w
    tn = min(cols, 2048)
    u2 = u.reshape(b, n2, cols)
    v2 = v.reshape(b, n2, cols)
    blk = pl.BlockSpec((1, n2, tn), lambda bi, j: (bi, 0, j))
    mat = pl.BlockSpec((2 * n2, n2), lambda bi, j: (0, 0))
    yr, yi = pl.pallas_call(
        _dft1_kernel,
        out_shape=[jax.ShapeDtypeStruct((b, n2, cols), BF16)] * 2,
        grid=(b, cols // tn),
        in_specs=[blk, blk, mat, mat],
        out_specs=[blk, blk],
        compiler_params=_cparams(("parallel", "parallel")),
        name="dft_stage1",
    )(u2, v2, m_u, m_v)
    yr = yr.reshape(b, n, w)
    yi = yi.reshape(b, n, w)
    rows = DFT_N1 * DFT_KGROUP
    groups = n2 // DFT_KGROUP
    yblk = pl.BlockSpec((1, rows, w), lambda bi, gi: (bi, gi, 0))
    mblk = pl.BlockSpec((1, rows, rows), lambda bi, gi: (gi, 0, 0))
    out = pl.pallas_call(
        _dft2_kernel,
        out_shape=jax.ShapeDtypeStruct((b, DFT_N1, n2, w), BF16),
        grid=(b, groups),
        in_specs=[yblk, yblk, mblk, mblk],
        out_specs=pl.BlockSpec((1, DFT_N1, DFT_KGROUP, w), lambda bi, gi: (bi, 0, gi, 0)),
        compiler_params=_cparams(("parallel", "parallel")),
        name="dft_stage2",
    )(yr, yi, a2, b2)
    return out.reshape(b, n, w)


def _attn_kernel(q_ref, k_ref, v_ref, lam_ref, g_ref, o_ref,
                 qs_ref, va_ref, m_ref, acc_ref, *, kv_chunks, tq):
    n_tiles = q_ref.shape[1] // tq
    n_keys = k_ref.shape[1]

    @pl.when(pl.program_id(2) == 0)
    def _():
        va_ref[:, 0:LANES] = v_ref[0]
        va_ref[:, LANES:] = jnp.ones((n_keys, LANES), BF16)

    lane = lax.broadcasted_iota(jnp.int32, (tq, LANES), 1)
    for t in range(n_tiles):
        q = q_ref[0, t * tq:(t + 1) * tq, :]
        zero = jnp.zeros_like(q)
        qs_ref[t, 0:tq, :] = jnp.where(lane < DIFF_HEAD_DIM, q, zero)
        qs_ref[t, tq:, :] = jnp.where(lane >= DIFF_HEAD_DIM, q, zero)
    m_ref[...] = jnp.full(m_ref.shape, -jnp.inf, F32)
    acc_ref[...] = jnp.zeros(acc_ref.shape, F32)

    def chunk(t, k, va):
        s = lax.dot_general(qs_ref[t], k, (((1,), (1,)), ((), ())), preferred_element_type=F32)
        cols = [s[:, c * LANES:(c + 1) * LANES] for c in range(k.shape[0] // LANES)]
        mx = cols[0]
        for col in cols[1:]:
            mx = jnp.maximum(mx, col)
        m_prev = m_ref[t]
        m_new = jnp.maximum(m_prev, jnp.max(mx, axis=1, keepdims=True))
        alpha = jnp.exp(m_prev - m_new)
        p = jnp.concatenate([jnp.exp(col - m_new).astype(BF16) for col in cols], axis=1)
        pv = jnp.dot(p, va, preferred_element_type=F32)
        acc_ref[t] = jnp.concatenate([alpha, alpha], axis=1) * acc_ref[t] + pv
        m_ref[t] = m_new

    start = 0
    for size in kv_chunks:
        for t in range(n_tiles):
            chunk(t, k_ref[0, start:start + size, :], va_ref[start:start + size, :])
        start += size

    lam = (jnp.exp(jnp.sum(lam_ref[0:1, :] * lam_ref[1:2, :], axis=-1, keepdims=True))
           - jnp.exp(jnp.sum(lam_ref[2:3, :] * lam_ref[3:4, :], axis=-1, keepdims=True)) + LAM_INIT)
    for t in range(n_tiles):
        o = (acc_ref[t, 0:tq, 0:LANES] / acc_ref[t, 0:tq, LANES:]
             - lam * (acc_ref[t, tq:, 0:LANES] / acc_ref[t, tq:, LANES:]))
        o_ref[0, t * tq:(t + 1) * tq, :] = (_rms(o) * g_ref[...] * (1.0 - LAM_INIT)).astype(BF16)


def _attention(q, k_all, v_all, lam_vecs, g_subln, tq, kv_chunks, q_tiles):
    b, n, _ = q.shape
    n_keys = k_all.shape[1]
    assert sum(kv_chunks) == n_keys and n % (tq * q_tiles) == 0
    qspec = pl.BlockSpec((1, tq * q_tiles, LANES), lambda bi, h, i: (bi, i, h))
    kspec = pl.BlockSpec((1, n_keys, LANES), lambda bi, h, i: (bi, 0, h))
    return pl.pallas_call(
        functools.partial(_attn_kernel, kv_chunks=kv_chunks, tq=tq),
        out_shape=jax.ShapeDtypeStruct((b, n, DIFF_V_WIDTH), BF16),
        grid=(b, DIFF_HEADS, n // (tq * q_tiles)),
        in_specs=[qspec, kspec, kspec,
                  pl.BlockSpec(lam_vecs.shape, lambda bi, h, i: (0, 0)),
                  pl.BlockSpec(g_subln.shape, lambda bi, h, i: (0, 0))],
        out_specs=qspec,
        scratch_shapes=[pltpu.VMEM((q_tiles, 2 * tq, LANES), BF16), pltpu.VMEM((n_keys, 2 * LANES), BF16),
                        pltpu.VMEM((q_tiles, 2 * tq, LANES), F32), pltpu.VMEM((q_tiles, 2 * tq, 2 * LANES), F32)],
        compiler_params=_cparams(("parallel", "parallel", "arbitrary")),
        name="diff_attention",
    )(q, k_all, v_all, lam_vecs, g_subln)


def _route(logits_t, bias_col):
    tm = logits_t.shape[1]
    scores = jax.nn.sigmoid(logits_t)
    sel = scores + bias_col
    ninf = jnp.float32(-jnp.inf)
    i8 = lax.broadcasted_iota(jnp.int32, (GROUP_SIZE, tm), 0)
    slabs = [sel[GROUP_SIZE * g:GROUP_SIZE * (g + 1)] for g in range(N_EXPERT_GROUPS)]
    rows = []
    for slab in slabs:
        m1 = jnp.max(slab, axis=0, keepdims=True)
        i1 = jnp.min(jnp.where(slab == m1, i8, GROUP_SIZE), axis=0, keepdims=True)
        m2 = jnp.max(jnp.where(i8 == i1, ninf, slab), axis=0, keepdims=True)
        rows.append(m1 + m2)
    gs = jnp.concatenate(rows, axis=0)
    gmask = jnp.zeros(gs.shape, jnp.bool_)
    for _ in range(TOP_K_GROUPS):
        m = jnp.max(gs, axis=0, keepdims=True)
        i = jnp.min(jnp.where(gs == m, i8, N_EXPERT_GROUPS), axis=0, keepdims=True)
        hit = i8 == i
        gmask = jnp.logical_or(gmask, hit)
        gs = jnp.where(hit, ninf, gs)
    gmask_f = gmask.astype(F32)
    cand = [jnp.where(gmask_f[g:g + 1] > 0.5, slabs[g], ninf) for g in range(N_EXPERT_GROUPS)]
    chosen = [jnp.zeros((GROUP_SIZE, tm), jnp.bool_) for _ in range(N_EXPERT_GROUPS)]
    picks = []
    for _ in range(TOP_K):
        mm = cand[0]
        for g in range(1, N_EXPERT_GROUPS):
            mm = jnp.maximum(mm, cand[g])
        m = jnp.max(mm, axis=0, keepdims=True)
        ii = jnp.where(cand[0] == m, i8, N_EXPERTS)
        for g in range(1, N_EXPERT_GROUPS):
            ii = jnp.minimum(ii, jnp.where(cand[g] == m, i8 + GROUP_SIZE * g, N_EXPERTS))
        idx = jnp.min(ii, axis=0, keepdims=True)
        picks.append(idx)
        for g in range(N_EXPERT_GROUPS):
            hit = (i8 + GROUP_SIZE * g) == idx
            chosen[g] = jnp.logical_or(chosen[g], hit)
            cand[g] = jnp.where(hit, ninf, cand[g])
    score_slabs = [scores[GROUP_SIZE * g:GROUP_SIZE * (g + 1)] for g in range(N_EXPERT_GROUPS)]
    tot = jnp.where(chosen[0], score_slabs[0], 0.0)
    for g in range(1, N_EXPERT_GROUPS):
        tot = tot + jnp.where(chosen[g], score_slabs[g], 0.0)
    denom = jnp.sum(tot, axis=0, keepdims=True) + 1e-20

    chosen_f = jnp.concatenate([c.astype(F32) for c in chosen], axis=0)
    chosen_b = chosen_f.astype(BF16)
    earlier = (lax.broadcasted_iota(jnp.int32, (tm, tm), 0) < lax.broadcasted_iota(jnp.int32, (tm, tm), 1))
    rank = jnp.dot(chosen_b, earlier.astype(BF16), preferred_element_type=F32)
    pad = lambda cnt: jnp.floor((cnt + (SEG - 1)) * (1.0 / SEG)) * SEG
    cp_col = pad(jnp.sum(chosen_f, axis=1, keepdims=True))
    lower = (lax.broadcasted_iota(jnp.int32, (N_EXPERTS, N_EXPERTS), 1)
             < lax.broadcasted_iota(jnp.int32, (N_EXPERTS, N_EXPERTS), 0))
    seg_start = jnp.dot(lower.astype(BF16), jnp.broadcast_to(cp_col, (N_EXPERTS, LANES)).astype(BF16),
                        preferred_element_type=F32)[:, 0:1]
    base = seg_start + rank
    base_slabs = [base[GROUP_SIZE * g:GROUP_SIZE * (g + 1)] for g in range(N_EXPERT_GROUPS)]

    def at_pick(slabs_, idx):
        acc = jnp.zeros((1, tm), F32)
        for g in range(N_EXPERT_GROUPS):
            acc = acc + jnp.sum(jnp.where((i8 + GROUP_SIZE * g) == idx, slabs_[g], 0.0), axis=0, keepdims=True)
        return acc

    pos = jnp.concatenate([at_pick(base_slabs, idx) for idx in picks], axis=0).astype(jnp.int32)
    wts = jnp.concatenate([at_pick(score_slabs, idx) / denom * ROUTED_SCALE for idx in picks], axis=0)
    cnt_row = lax.dot_general(jnp.ones((GROUP_SIZE, tm), BF16), chosen_b, (((1,), (1,)), ((), ())),
                              preferred_element_type=F32)
    return pos, wts, pad(cnt_row).astype(jnp.int32)


def _mix_kernel(four_ref, att_ref, x_ref, mod_ref, gpost_ref, gpre_ref, wo_ref, wr_ref, rb_ref,
                wsg_ref, wsu_ref, wsd_ref, x1_ref, h2_ref, sh_ref, pos_ref, wts_ref, cp_ref):
    wr = wr_ref[...]
    wr_hi = wr.astype(BF16)
    wr_lo = (wr - wr_hi.astype(F32)).astype(BF16)
    tiles = range(x_ref.shape[1] // MOE_TILE)
    rows = [slice(t * MOE_TILE, (t + 1) * MOE_TILE) for t in tiles]
    h2s, hbs, logits = [], [], []
    for t in tiles:
        y = (jnp.dot(four_ref[0, rows[t], :], wo_ref[0:FOURIER_WIDTH, :], preferred_element_type=F32)
             + jnp.dot(att_ref[0, rows[t], :], wo_ref[FOURIER_WIDTH:, :], preferred_element_type=F32))
        x1 = x_ref[0, rows[t], :] + mod_ref[0, 2:3, :] * (_rms(y) * gpost_ref[...])
        x1_ref[0, rows[t], :] = x1
        h2 = _rms(x1) * gpre_ref[...]
        h2 = h2 * (1.0 + mod_ref[0, 4:5, :]) + mod_ref[0, 3:4, :]
        h2s.append(h2)
        hbs.append(h2.astype(BF16))
        h2_ref[0, rows[t], :] = hbs[t]
    for t in tiles:
        h_lo = (h2s[t] - hbs[t].astype(F32)).astype(BF16)
        logits.append(jnp.dot(hbs[t], wr_hi, preferred_element_type=F32)
                      + jnp.dot(hbs[t], wr_lo, preferred_element_type=F32)
                      + jnp.dot(h_lo, wr_hi, preferred_element_type=F32))
    for t in tiles:
        pos_ref[t], wts_ref[t], cp_ref[t] = _route(logits[t].T, rb_ref[...])
        a = jnp.dot(hbs[t], wsg_ref[...], preferred_element_type=F32)
        u = jnp.dot(hbs[t], wsu_ref[...], preferred_element_type=F32)
        hid = (a * jax.nn.sigmoid(a)) * u
        sh_ref[0, rows[t], :] = jnp.dot(hid.astype(BF16), wsd_ref[...], preferred_element_type=F32)


def _mix(four, att, x, mod, g_post, g_pre, w_out_b, w_router, rb_col, wsg, wsu, wsd):
    b, n, d = x.shape
    sub = MIX_TILES_PER_STEP if n % (MIX_TILES_PER_STEP * MOE_TILE) == 0 else 1
    tm = sub * MOE_TILE
    steps = n // tm
    tok = lambda w: pl.BlockSpec((1, tm, w), lambda bi, i: (bi, i, 0))
    full = lambda a: pl.BlockSpec(a.shape, lambda bi, i: (0,) * a.ndim)
    plan = lambda r, w: pl.BlockSpec((sub, r, w), lambda bi, i: (bi * steps + i, 0, 0))
    n_tiles = b * n // MOE_TILE
    return pl.pallas_call(
        _mix_kernel,
        out_shape=[jax.ShapeDtypeStruct((b, n, d), F32), jax.ShapeDtypeStruct((b, n, d), BF16),
                   jax.ShapeDtypeStruct((b, n, d), F32),
                   jax.ShapeDtypeStruct((n_tiles, TOP_K, MOE_TILE), jnp.int32),
                   jax.ShapeDtypeStruct((n_tiles, TOP_K, MOE_TILE), F32),
                   jax.ShapeDtypeStruct((n_tiles, GROUP_SIZE, N_EXPERTS), jnp.int32)],
        grid=(b, steps),
        in_specs=[tok(FOURIER_WIDTH), tok(DIFF_V_WIDTH), tok(d), pl.BlockSpec((1, N_MOD, d), lambda bi, i: (bi, 0, 0)),
                  full(g_post), full(g_pre), full(w_out_b), full(w_router), full(rb_col), full(wsg), full(wsu), full(wsd)],
        out_specs=[tok(d), tok(d), tok(d), plan(TOP_K, MOE_TILE), plan(TOP_K, MOE_TILE), plan(GROUP_SIZE, N_EXPERTS)],
        compiler_params=_cparams(("parallel", "parallel")),
        name="mix",
    )(four, att, x, mod, g_post, g_pre, w_out_b, w_router, rb_col, wsg, wsu, wsd)


def _moe_plan(cp):
    n_tiles = cp.shape[0]
    ends_loc = jnp.cumsum(cp, axis=1)
    seg_loc = ends_loc - cp
    reg = (jnp.sum(cp, axis=0) + MOE_BLOCK - 1) // MOE_BLOCK * MOE_BLOCK
    reg_end = jnp.cumsum(reg)
    seg_glob = (reg_end - reg)[None, :] + jnp.cumsum(cp, axis=0) - cp
    g_row = jnp.arange(LOC_ROWS // SEG, dtype=jnp.int32) * SEG
    e_of_g = jnp.sum((ends_loc[:, None, :] <= g_row[None, :, None]).astype(jnp.int32), axis=-1)
    owner = jnp.minimum(e_of_g, N_EXPERTS - 1)[:, :, None] == jnp.arange(N_EXPERTS, dtype=jnp.int32)[None, None, :]
    delta = jnp.sum(jnp.where(owner, (seg_glob - seg_loc)[:, None, :], 0), axis=-1)
    n_glob = _moe_rows(n_tiles * MOE_TILE)
    gdst = jnp.clip((delta + g_row[None, :]) // SEG, 0, n_glob // SEG - 1)
    ngran = ends_loc[:, -1] // SEG
    n_blocks = reg_end[-1] // MOE_BLOCK
    blk = jnp.minimum(jnp.arange(n_glob // MOE_BLOCK, dtype=jnp.int32), n_blocks - 1) * MOE_BLOCK
    blk_expert = jnp.minimum(jnp.sum((reg_end[None, :] <= blk[:, None]).astype(jnp.int32), axis=1), N_EXPERTS - 1)
    seg_tot = jnp.sum(cp, axis=0)
    k = jnp.arange(TAIL_SLOTS, dtype=jnp.int32)
    tail_gran = jnp.clip(((reg_end - reg + seg_tot) // SEG)[:, None] + k[None, :], 0, n_glob // SEG - 1)
    tail_ok = k[None, :] < ((reg - seg_tot) // SEG)[:, None]
    i32 = lambda a: a.astype(jnp.int32)
    return {"gdst": i32(gdst.reshape(-1)), "ngran": i32(ngran), "blk_expert": i32(blk_expert),
            "n_blocks": i32(n_blocks.reshape(1)), "tail_gran": i32(tail_gran.reshape(-1)),
            "tail_ok": i32(tail_ok.reshape(-1)), "n_tail": i32(jnp.sum(tail_ok).reshape(1))}


def _moe_rows(n_tok):
    rows = n_tok * TOP_K + (n_tok // MOE_TILE) * N_EXPERTS * (SEG - 1) + N_EXPERTS * (MOE_BLOCK - 1)
    return -(-rows // MOE_BLOCK) * MOE_BLOCK


def _for_granules(n, fn):
    groups = lax.shift_right_logical(n, DMA_UNROLL.bit_length() - 1)

    def group(k, c):
        for u in range(DMA_UNROLL):
            fn(k * DMA_UNROLL + u)
        return c

    lax.fori_loop(0, groups, group, 0)
    lax.fori_loop(groups * DMA_UNROLL, n, lambda g, c: (fn(g), c)[1], 0)


def _wait_granules(n, bulk, single):
    groups = lax.shift_right_logical(n, DMA_UNROLL.bit_length() - 1)
    lax.fori_loop(0, groups, lambda g, c: (bulk.wait(), c)[1], 0)
    lax.fori_loop(groups * DMA_UNROLL, n, lambda g, c: (single.wait(), c)[1], 0)


def _dispatch_kernel(gdst_ref, ngran_ref, tail_gran_ref, tail_ok_ref, n_tail_ref, pos_ref, posn_ref, h_ref, xs_ref,
                     xloc_ref, p_ref, zero_ref, sem_ref):
    i = pl.program_id(0)
    last = pl.num_programs(0) - 1
    buf = lax.rem(i, 2)
    gl = LOC_ROWS // SEG

    def granule(b, g_loc, g_glob):
        return pltpu.make_async_copy(xloc_ref.at[b, pl.ds(g_loc * SEG, SEG), :],
                                     xs_ref.at[pl.ds(g_glob * SEG, SEG), :], sem_ref.at[b])

    def tail_granule(g_glob):
        return pltpu.make_async_copy(zero_ref, xs_ref.at[pl.ds(g_glob * SEG, SEG), :], sem_ref.at[2])

    def wait_tile(tile, b):
        bulk = pltpu.make_async_copy(xloc_ref.at[b, pl.ds(0, DMA_UNROLL * SEG), :],
                                     xs_ref.at[pl.ds(0, DMA_UNROLL * SEG), :], sem_ref.at[b])
        _wait_granules(ngran_ref[tile], bulk, granule(b, 0, 0))

    @pl.when(i == 0)
    def _():
        zero_ref[...] = jnp.zeros(zero_ref.shape, BF16)

        def body(s, c):
            @pl.when(tail_ok_ref[s] == 1)
            def _():
                tail_granule(tail_gran_ref[s]).start()

            return c

        lax.fori_loop(0, N_EXPERTS * TAIL_SLOTS, body, 0)

    @pl.when(i >= 2)
    def _():
        wait_tile(i - 2, buf)

    def build_onehot(pos, slot):
        for rb in range(LOC_ROWS // LOC_BLOCK):
            r = lax.broadcasted_iota(jnp.int32, (LOC_BLOCK, MOE_TILE), 0) + rb * LOC_BLOCK
            r16 = r.astype(jnp.int16)
            pos16 = pos.astype(jnp.int16)
            onehot = jnp.zeros((LOC_BLOCK, MOE_TILE), BF16)
            for j in range(TOP_K):
                onehot = jnp.where(r16 == pos16[j:j + 1, :], jnp.ones_like(onehot), onehot)
            p_ref[slot, rb * LOC_BLOCK:(rb + 1) * LOC_BLOCK, :] = onehot

    @pl.when(i == 0)
    def _():
        build_onehot(pos_ref[0], 0)

    build_onehot(posn_ref[0], 1 - buf)
    h = h_ref[...]
    for rb in range(LOC_ROWS // LOC_BLOCK):
        rows = slice(rb * LOC_BLOCK, (rb + 1) * LOC_BLOCK)
        xloc_ref[buf, rows, :] = jnp.dot(p_ref[buf, rows, :], h, preferred_element_type=F32).astype(BF16)

    _for_granules(ngran_ref[i], lambda g: granule(buf, g, gdst_ref[i * gl + g]).start())

    @pl.when(i == last)
    def _():
        wait_tile(i, buf)

        @pl.when(i >= 1)
        def _():
            wait_tile(i - 1, 1 - buf)

        lax.fori_loop(0, n_tail_ref[0], lambda s, c: (tail_granule(0).wait(), c)[1], 0)


def _dispatch(h2, pos, plan):
    n_tok, d = h2.shape
    n_tiles = n_tok // MOE_TILE
    grid_spec = pltpu.PrefetchScalarGridSpec(
        num_scalar_prefetch=5,
        grid=(n_tiles,),
        in_specs=[pl.BlockSpec((1, TOP_K, MOE_TILE), lambda i, *_: (i, 0, 0)),
                  pl.BlockSpec((1, TOP_K, MOE_TILE), lambda i, *_: (jnp.minimum(i + 1, n_tiles - 1), 0, 0)),
                  pl.BlockSpec((MOE_TILE, d), lambda i, *_: (i, 0))],
        out_specs=pl.BlockSpec(memory_space=pl.ANY),
        scratch_shapes=[pltpu.VMEM((2, LOC_ROWS, d), BF16), pltpu.VMEM((2, LOC_ROWS, MOE_TILE), BF16),
                        pltpu.VMEM((SEG, d), BF16), pltpu.SemaphoreType.DMA((3,))],
    )
    return pl.pallas_call(
        _dispatch_kernel,
        out_shape=jax.ShapeDtypeStruct((_moe_rows(n_tok), d), BF16),
        grid_spec=grid_spec,
        compiler_params=_cparams(("arbitrary",)),
        name="moe_dispatch",
    )(plan["gdst"], plan["ngran"], plan["tail_gran"], plan["tail_ok"], plan["n_tail"], pos, pos, h2)


def _expert_kernel(be_ref, nb_ref, x_ref, wg_ref, wu_ref, wd_ref, y_ref, wgb_ref, wub_ref, wdb_ref):
    i = pl.program_id(0)

    @pl.when(jnp.logical_or(i == 0, be_ref[i] != be_ref[jnp.maximum(i - 1, 0)]))
    def _():
        wgb_ref[...] = wg_ref[0].astype(BF16)
        wub_ref[...] = wu_ref[0].astype(BF16)
        wdb_ref[...] = wd_ref[0].astype(BF16)

    @pl.when(i < nb_ref[0])
    def _():
        x = x_ref[...]
        a = jnp.dot(x, wgb_ref[...], preferred_element_type=F32)
        u = jnp.dot(x, wub_ref[...], preferred_element_type=F32)
        hid = (a * jax.nn.sigmoid(a)) * u
        y_ref[...] = jnp.dot(hid.astype(BF16), wdb_ref[...], preferred_element_type=F32).astype(BF16)


def _experts(xs, blk_expert, n_blocks, wg, wu, wd):
    rows, d = xs.shape
    de = wg.shape[-1]
    live = lambda i, be, nb: jnp.maximum(jnp.minimum(i, nb[0] - 1), 0)
    grid_spec = pltpu.PrefetchScalarGridSpec(
        num_scalar_prefetch=2,
        grid=(rows // MOE_BLOCK,),
        in_specs=[pl.BlockSpec((MOE_BLOCK, d), lambda i, be, nb: (live(i, be, nb), 0)),
                  pl.BlockSpec((1, d, de), lambda i, be, nb: (be[i], 0, 0)),
                  pl.BlockSpec((1, d, de), lambda i, be, nb: (be[i], 0, 0)),
                  pl.BlockSpec((1, de, d), lambda i, be, nb: (be[i], 0, 0))],
        out_specs=pl.BlockSpec((MOE_BLOCK, d), lambda i, be, nb: (live(i, be, nb), 0)),
        scratch_shapes=[pltpu.VMEM((d, de), BF16), pltpu.VMEM((d, de), BF16), pltpu.VMEM((de, d), BF16)],
    )
    return pl.pallas_call(
        _expert_kernel,
        out_shape=jax.ShapeDtypeStruct((rows, d), BF16),
        grid_spec=grid_spec,
        compiler_params=_cparams(("arbitrary",)),
        name="moe_experts",
    )(blk_expert, n_blocks, xs, wg, wu, wd)


def _combine_kernel(gdst_ref, ngran_ref, ys_ref, pos_ref, wts_ref, posn_ref, wtsn_ref, sh_ref, x1_ref, mod_ref, g_ref,
                    o_ref, yloc_ref, w_ref, sem_ref):
    i = pl.program_id(0)
    last = pl.num_programs(0) - 1
    buf = lax.rem(i, 2)
    gl = LOC_ROWS // SEG

    def granule(b, g_loc, g_glob):
        return pltpu.make_async_copy(ys_ref.at[pl.ds(g_glob * SEG, SEG), :],
                                     yloc_ref.at[b, pl.ds(g_loc * SEG, SEG), :], sem_ref.at[b])

    def fetch(tile, b):
        _for_granules(ngran_ref[tile], lambda g: granule(b, g, gdst_ref[tile * gl + g]).start())

    @pl.when(i == 0)
    def _():
        yloc_ref[...] = jnp.zeros(yloc_ref.shape, BF16)
        fetch(0, 0)

    @pl.when(i < last)
    def _():
        fetch(i + 1, 1 - buf)

    bulk = pltpu.make_async_copy(ys_ref.at[pl.ds(0, DMA_UNROLL * SEG), :],
                                 yloc_ref.at[buf, pl.ds(0, DMA_UNROLL * SEG), :], sem_ref.at[buf])
    _wait_granules(ngran_ref[i], bulk, granule(buf, 0, 0))

    def build_weights(pos, wts, slot):
        lane = lax.broadcasted_iota(jnp.int32, (MOE_TILE, LANES), 1)
        pos_b = [jnp.broadcast_to(pos[:, j:j + 1], (MOE_TILE, LANES)).astype(jnp.int16) for j in range(TOP_K)]
        wts_b = [jnp.broadcast_to(wts[:, j:j + 1], (MOE_TILE, LANES)).astype(BF16) for j in range(TOP_K)]
        for c in range(LOC_ROWS // LANES):
            target = (lane + c * LANES).astype(jnp.int16)
            w = jnp.zeros((MOE_TILE, LANES), BF16)
            for j in range(TOP_K):
                w = jnp.where(pos_b[j] == target, wts_b[j], w)
            w_ref[slot, :, c * LANES:(c + 1) * LANES] = w

    @pl.when(i == 0)
    def _():
        build_weights(pos_ref[...], wts_ref[...], 0)

    build_weights(posn_ref[...], wtsn_ref[...], 1 - buf)
    routed = jnp.dot(w_ref[buf], yloc_ref[buf], preferred_element_type=F32)
    o_ref[...] = x1_ref[...] + mod_ref[0, 5:6, :] * (_rms(routed + sh_ref[...]) * g_ref[...])


def _combine(ys, pos_t, wts_t, shared, x1, mod, g_post, gdst, ngran, tiles_per_batch):
    n_tok, d = x1.shape
    n_tiles = n_tok // MOE_TILE
    tok = lambda w: pl.BlockSpec((MOE_TILE, w), lambda i, *_: (i, 0))
    nxt = lambda w: pl.BlockSpec((MOE_TILE, w), lambda i, *_: (jnp.minimum(i + 1, n_tiles - 1), 0))
    grid_spec = pltpu.PrefetchScalarGridSpec(
        num_scalar_prefetch=2,
        grid=(n_tiles,),
        in_specs=[pl.BlockSpec(memory_space=pl.ANY), tok(TOP_K), tok(TOP_K), nxt(TOP_K), nxt(TOP_K), tok(d), tok(d),
                  pl.BlockSpec((1, N_MOD, d), lambda i, *_: (i // tiles_per_batch, 0, 0)),
                  pl.BlockSpec(g_post.shape, lambda i, *_: (0, 0))],
        out_specs=tok(d),
        scratch_shapes=[pltpu.VMEM((2, LOC_ROWS, d), BF16), pltpu.VMEM((2, MOE_TILE, LOC_ROWS), BF16),
                        pltpu.SemaphoreType.DMA((2,))],
    )
    return pl.pallas_call(
        _combine_kernel,
        out_shape=jax.ShapeDtypeStruct((n_tok, d), F32),
        grid_spec=grid_spec,
        compiler_params=_cparams(("arbitrary",)),
        name="moe_combine",
    )(gdst, ngran, ys, pos_t, wts_t, pos_t, wts_t, shared, x1, mod, g_post)


def kernel(x, c, ctx, c_ctx, w_mod, b_mod, g_attn_pre, g_attn_post, w_in, w_fourier, lambda_q1, lambda_k1,
           lambda_q2, lambda_k2, g_subln, w_out, g_ffn_pre, g_ffn_post, w_router, router_bias, w_gate, w_up,
           w_down, ws_gate, ws_up, ws_down):
    b, n, d = x.shape
    assert w_mod.shape[0] == 1, "single-layer kernel"
    assert n % (DFT_N1 * DFT_KGROUP) == 0 and b + 1 <= 8
    row = lambda a: a[0].reshape(1, -1)

    cond8 = jnp.zeros((8, d), F32).at[:b].set(c).at[b].set(c_ctx)
    mod = _ada(cond8, w_mod[0], b_mod[0]).reshape(8, N_MOD, d)

    w_in_b = w_in[0].astype(BF16)
    cc, sc = _channel_dft(n)
    wf = jnp.zeros((FOURIER_GROUPS, FOURIER_GROUP_DIM, FOURIER_GROUPS, FOURIER_GROUP_DIM), F32)
    for g in range(FOURIER_GROUPS):
        wf = wf.at[g, :, g, :].set(w_fourier[0, g])
    wf = wf.reshape(FOURIER_WIDTH, FOURIER_WIDTH).astype(BF16)

    tiles = _tiles(n, ctx.shape[1])
    u, v2, q, k_all, v_all = _prep(x, ctx, mod, b, row(g_attn_pre), w_in_b, _rope_tables(n), cc, sc, wf)

    m_u, m_v = _dft_stage1(n // DFT_N1)
    a2, b2 = _dft_stage2(n)
    four = _fourier(u, v2, m_u, m_v, a2, b2)

    lam_vecs = jnp.concatenate([lambda_q1, lambda_k1, lambda_q2, lambda_k2], axis=0).astype(F32)
    att = _attention(q, k_all, v_all, lam_vecs, row(g_subln), tiles["attn_q"], tiles["attn_kv"],
                     tiles["attn_q_tiles"])

    x1, h2, shared, pos, wts, cp = _mix(four, att, x, mod, row(g_attn_post), row(g_ffn_pre), w_out[0].astype(BF16),
                                        w_router[0], router_bias[0].reshape(-1, 1), ws_gate[0].astype(BF16),
                                        ws_up[0].astype(BF16), ws_down[0].astype(BF16))
    plan = _moe_plan(cp[:, 0, :])
    xs = _dispatch(h2.reshape(b * n, d), pos, plan)
    ys = _experts(xs, plan["blk_expert"], plan["n_blocks"], w_gate[0], w_up[0], w_down[0])
    token_major = lambda a: a.transpose(0, 2, 1).reshape(b * n, TOP_K)
    out = _combine(ys, token_major(pos), token_major(wts), shared.reshape(b * n, d), x1.reshape(b * n, d), mod,
                   row(g_ffn_post), plan["gdst"], plan["ngran"], n // MOE_TILE)
    return out.reshape(b, n, d)
```
